```python
import math
import jax, jax.numpy as jnp
from jax import lax
import numpy as np

D_MODEL = 1024
BATCH = 8
SEQ = 4096
DEPTH = 4

D_BRANCH = D_MODEL
N_BRANCH = 3
CONV_K = 31
HGRN_HEAD = 128
HGRN_HEADS = D_BRANCH // HGRN_HEAD
HGRN_CHUNK = 64
FOX_HEAD = 64
FOX_HEADS = D_BRANCH // FOX_HEAD
Q_BLOCK = 128
EPS = 1e-6

SPLIT_SIZES = (D_BRANCH, D_BRANCH, D_BRANCH,
               D_BRANCH, D_BRANCH, D_BRANCH, D_BRANCH,
               D_BRANCH, D_BRANCH, D_BRANCH, D_BRANCH, FOX_HEADS,
               N_BRANCH * D_MODEL)
N_IN = sum(SPLIT_SIZES)

kernel_name = 'hybrid_conv_hgrn2_fox_gated'


def rms_norm(x, g):
    xf = x.astype(jnp.float32)
    y = xf * lax.rsqrt(jnp.mean(xf * xf, axis=-1, keepdims=True) + EPS)
    return (y * g.astype(jnp.float32)).astype(x.dtype)


def layer_norm(x, g, b):
    xf = x.astype(jnp.float32)
    mu = jnp.mean(xf, axis=-1, keepdims=True)
    var = jnp.mean(jnp.square(xf - mu), axis=-1, keepdims=True)
    y = (xf - mu) * lax.rsqrt(var + EPS)
    return (y * g.astype(jnp.float32) + b.astype(jnp.float32)).astype(x.dtype)


def conformer_conv(val, glu_gate, conv_w, conv_b, ln_g, ln_b):
    u = val * jax.nn.sigmoid(glu_gate)
    y = lax.conv_general_dilated(
        u, conv_w[:, None, :].astype(u.dtype), window_strides=(1,),
        padding=[(CONV_K - 1, 0)],
        dimension_numbers=('NWC', 'WIO', 'NWC'),
        feature_group_count=D_BRANCH) + conv_b
    return jax.nn.silu(layer_norm(y, ln_g, ln_b))


def hgrn2(q, f_pre, i, lb, norm_g):
    B, S, _ = q.shape
    H, Dh, C = HGRN_HEADS, HGRN_HEAD, HGRN_CHUNK
    nc = S // C
    f32 = jnp.float32
    qf = jax.nn.silu(q.astype(f32))
    z = f_pre.astype(f32)
    lbf = lb.astype(f32)
    f = lbf + (1.0 - lbf) * jax.nn.sigmoid(z)
    k = 1.0 - f
    log_f = jnp.logaddexp(jnp.log(lbf), jnp.log1p(-lbf) + jax.nn.log_sigmoid(z))
    v = i.astype(f32)

    def to_chunks(t):
        return t.reshape(B, nc, C, H, Dh).transpose(1, 0, 3, 2, 4)

    causal = jnp.tril(jnp.ones((C, C), dtype=bool))

    def step(state, xs):
        qc, kc, vc, gc = xs
        b = jnp.cumsum(gc, axis=2)
        diff = b[:, :, :, None, :] - b[:, :, None, :, :]
        decay = jnp.exp(jnp.where(causal[:, :, None], diff, -jnp.inf))
        scores = jnp.einsum('bhtd,bhsd,bhtsd->bhts', qc, kc, decay)
        o = (jnp.einsum('bhts,bhsv->bhtv', scores, vc)
             + jnp.einsum('bhtd,bhdv->bhtv', qc * jnp.exp(b), state))
        b_last = b[:, :, -1:, :]
        new_state = (jnp.exp(b_last[:, :, 0, :])[..., None] * state
                     + jnp.einsum('bhsd,bhsv->bhdv', kc * jnp.exp(b_last - b), vc))
        return new_state, o

    s0 = jnp.zeros((B, H, Dh, Dh), f32)
    _, o = lax.scan(step, s0, (to_chunks(qf), to_chunks(k), to_chunks(v), to_chunks(log_f)))
    o = o.transpose(1, 0, 3, 2, 4).reshape(B, S, H, Dh)
    o = o * lax.rsqrt(jnp.mean(o * o, axis=-1, keepdims=True) + EPS)
    o = o * norm_g.astype(f32).reshape(H, Dh)
    return o.reshape(B, S, D_BRANCH).astype(q.dtype)


def forgetting_attention(q, k, v, f_logit, qn_g, kn_g):
    B, S, _ = q.shape
    H, Dh = FOX_HEADS, FOX_HEAD
    f32 = jnp.float32
    qh = rms_norm(q.reshape(B, S, H, Dh), qn_g).transpose(0, 2, 1, 3)
    kh = rms_norm(k.reshape(B, S, H, Dh), kn_g).transpose(0, 2, 1, 3)
    vh = v.reshape(B, S, H, Dh).transpose(0, 2, 1, 3)
    F = jnp.cumsum(jax.nn.log_sigmoid(f_logit.astype(f32)), axis=1).transpose(0, 2, 1)
    nb = S // Q_BLOCK
    q_blocks = qh.reshape(B, H, nb, Q_BLOCK, Dh).transpose(2, 0, 1, 3, 4)
    F_blocks = F.reshape(B, H, nb, Q_BLOCK).transpose(2, 0, 1, 3)
    key_pos = jnp.arange(S)
    scale = 1.0 / math.sqrt(Dh)

    def block(args):
        qb, Fb, blk = args
        logits = jnp.einsum('bhqd,bhkd->bhqk', qb, kh).astype(f32) * scale
        logits = logits + Fb[..., None] - F[:, :, None, :]
        q_pos = blk * Q_BLOCK + jnp.arange(Q_BLOCK)
        logits = jnp.where(key_pos[None, :] <= q_pos[:, None], logits, -jnp.inf)
        p = jax.nn.softmax(logits, axis=-1).astype(vh.dtype)
        return jnp.einsum('bhqk,bhkd->bhqd', p, vh)

    out = lax.map(block, (q_blocks, F_blocks, jnp.arange(nb)))
    return out.transpose(1, 0, 3, 2, 4).reshape(B, S, D_BRANCH)


def setup_inputs(seed: int = 0) -> dict:
    key = jax.random.key(seed)
    ks = jax.random.split(key, 15)
    nrm = jax.random.normal
    return {
        'x': nrm(ks[0], (BATCH, SEQ, D_MODEL), jnp.float32),
        'norm_g': 1.0 + 0.02 * nrm(ks[1], (DEPTH, D_MODEL), jnp.float32),
        'w_in': nrm(ks[2], (DEPTH, D_MODEL, N_IN), jnp.float32) * D_MODEL ** -0.5,
        'conv_w': nrm(ks[3], (DEPTH, CONV_K, D_BRANCH), jnp.float32) * CONV_K ** -0.5,
        'conv_b': 0.02 * nrm(ks[4], (DEPTH, D_BRANCH), jnp.float32),
        'conv_ln_g': 1.0 + 0.02 * nrm(ks[5], (DEPTH, D_BRANCH), jnp.float32),
        'conv_ln_b': 0.02 * nrm(ks[6], (DEPTH, D_BRANCH), jnp.float32),
        'hgrn_lb_logits': 0.5 * nrm(ks[7], (DEPTH, D_BRANCH), jnp.float32),
        'hgrn_norm_g': 1.0 + 0.02 * nrm(ks[8], (DEPTH, D_BRANCH), jnp.float32),
        'fox_f_bias': 2.0 + 0.1 * nrm(ks[9], (DEPTH, FOX_HEADS), jnp.float32),
        'fox_qn_g': 1.0 + 0.02 * nrm(ks[10], (DEPTH, FOX_HEAD), jnp.float32),
        'fox_kn_g': 1.0 + 0.02 * nrm(ks[11], (DEPTH, FOX_HEAD), jnp.float32),
        'w_branch': nrm(ks[12], (DEPTH, N_BRANCH, D_BRANCH, D_MODEL), jnp.float32) * D_BRANCH ** -0.5,
        'w_out': nrm(ks[13], (DEPTH, D_MODEL, D_MODEL), jnp.float32) * (0.5 * D_MODEL ** -0.5),
    }


def reference(x, norm_g, w_in, conv_w, conv_b, conv_ln_g, conv_ln_b, hgrn_lb_logits,
              hgrn_norm_g, fox_f_bias, fox_qn_g, fox_kn_g, w_branch, w_out):
    B, S, _ = x.shape
    split_points = np.cumsum(SPLIT_SIZES)[:-1].tolist()
    p = jax.nn.softmax(hgrn_lb_logits.astype(jnp.float32), axis=0)
    cum = jnp.cumsum(p, axis=0)
    lower_bounds = cum - cum[0:1]
    for l in range(DEPTH):
        h = rms_norm(x, norm_g[l])
        proj = jnp.einsum('bsd,dn->bsn', h, w_in[l])
        (a_val, a_glu, a_gate, b_q, b_f, b_i, b_gate,
         c_q, c_k, c_v, c_gate, c_f, merge) = jnp.split(proj, split_points, axis=-1)
        ya = conformer_conv(a_val, a_glu, conv_w[l], conv_b[l], conv_ln_g[l], conv_ln_b[l]) * jax.nn.silu(a_gate)
        yb = hgrn2(b_q, b_f, b_i, lower_bounds[l], hgrn_norm_g[l]) * jax.nn.silu(b_gate)
        yc = forgetting_attention(c_q, c_k, c_v, c_f + fox_f_bias[l], fox_qn_g[l], fox_kn_g[l]) * jax.nn.silu(c_gate)
        branches = jnp.stack([ya, yb, yc], axis=2)
        branch_d = jnp.einsum('bskc,kcd->bskd', branches, w_branch[l])
        gates = jax.nn.sigmoid(merge.reshape(B, S, N_BRANCH, D_MODEL))
        mixed = jnp.sum(gates * branch_d, axis=2)
        x = x + jnp.einsum('bsd,de->bse', mixed, w_out[l])
    return x
```

```python
import functools
import math

import jax
import jax.numpy as jnp
import numpy as np
from jax import lax
from jax.experimental import pallas as pl
from jax.experimental.pallas import tpu as pltpu

F32 = jnp.float32
BF16 = jnp.bfloat16

D_MODEL = 1024
D_BRANCH = 1024
N_BRANCH = 3
CONV_K = 31
HGRN_HEAD = 128
HGRN_CHUNK = 64
FOX_HEAD = 64
FOX_HEADS = D_BRANCH // FOX_HEAD
EPS = 1e-6

LANES = 128
SUBLANES = 8
VMEM_LIMIT = 56 * 1024 * 1024

COL_A_VAL, COL_A_GLU, COL_A_GATE = 0, 1, 2
COL_B_Q, COL_B_F, COL_B_I, COL_B_GATE = 3, 4, 5, 6
COL_C_Q, COL_C_K, COL_C_V, COL_C_GATE = 7, 8, 9, 10
COL_MERGE = 11
N_MAIN = 14 * D_BRANCH
C_F_OFFSET = 11 * D_BRANCH

CONV_HALO = 32
N_SPLIT = 3
AUG_ONE_LANE = 3 * FOX_HEADS


def _sigmoid(x):
    return 1.0 / (1.0 + jnp.exp(-x))


def _silu(x):
    return x * _sigmoid(x)


def _log_sigmoid(x):
    return jnp.minimum(x, 0.0) - jnp.log1p(jnp.exp(-jnp.abs(x)))


def _split_bf16(x, n):
    pieces = []
    r = x
    for _ in range(n):
        p = r.astype(BF16)
        pieces.append(p)
        r = r - p.astype(F32)
    return pieces


def _params(*sem):
    return pltpu.CompilerParams(dimension_semantics=sem, vmem_limit_bytes=VMEM_LIMIT)


def _inproj_body(x_ref, g_ref, w_ref, wf_ref, proj_ref, f_ref, h_ref):
    @pl.when(pl.program_id(1) == 0)
    def _():
        x = x_ref[...]
        ms = jnp.mean(x * x, axis=-1, keepdims=True)
        h = (x * lax.rsqrt(ms + EPS) * g_ref[...]).astype(BF16)
        h_ref[...] = h
        f_ref[...] = jnp.dot(h, wf_ref[...], preferred_element_type=F32)

    proj_ref[...] = jnp.dot(h_ref[...], w_ref[...], preferred_element_type=F32).astype(BF16)


def _inproj(x2, g, w_main, w_f, tm, tn):
    t = x2.shape[0]
    return pl.pallas_call(
        _inproj_body,
        grid=(t // tm, N_MAIN // tn),
        in_specs=[
            pl.BlockSpec((tm, D_MODEL), lambda m, n: (m, 0)),
            pl.BlockSpec((1, D_MODEL), lambda m, n: (0, 0)),
            pl.BlockSpec((D_MODEL, tn), lambda m, n: (0, n)),
            pl.BlockSpec((D_MODEL, LANES), lambda m, n: (0, 0)),
        ],
        out_specs=[
            pl.BlockSpec((tm, tn), lambda m, n: (m, n)),
            pl.BlockSpec((tm, LANES), lambda m, n: (m, 0)),
        ],
        out_shape=[
            jax.ShapeDtypeStruct((t, N_MAIN), BF16),
            jax.ShapeDtypeStruct((t, LANES), F32),
        ],
        scratch_shapes=[pltpu.VMEM((tm, D_MODEL), BF16)],
        compiler_params=_params("parallel", "arbitrary"),
        name="inproj",
    )(x2, g, w_main, w_f)


def _conv_body(val_ref, glu_ref, gate_ref, w_ref, cb_ref, lg_ref, lb_ref, o_ref, u_ref, *, ts, rc):
    @pl.when(pl.program_id(1) == 0)
    def _():
        u_ref[0, 0:CONV_HALO, :] = jnp.zeros((CONV_HALO, D_BRANCH), F32)

    u_ref[0, CONV_HALO:CONV_HALO + ts, :] = val_ref[...].astype(F32) * _sigmoid(glu_ref[...].astype(F32))
    blk = 5 * SUBLANES
    for b in range(1, SUBLANES):
        for r in range(SUBLANES, CONV_HALO + ts, blk):
            n = min(blk, CONV_HALO + ts - r)
            u_ref[b, r:r + n, :] = u_ref[0, r - b:r - b + n, :]

    def chunk(c, carry):
        r0 = pl.multiple_of(c * rc, rc)
        acc = jnp.broadcast_to(cb_ref[...], (rc, D_BRANCH))
        for j in range(CONV_K):
            back = CONV_K - 1 - j
            a, b = divmod(back, SUBLANES)
            acc = acc + w_ref[j:j + 1, :] * u_ref[b, pl.ds(r0 + CONV_HALO - SUBLANES * a, rc), :]
        mu = jnp.mean(acc, axis=-1, keepdims=True)
        d = acc - mu
        var = jnp.mean(d * d, axis=-1, keepdims=True)
        y = d * lax.rsqrt(var + EPS) * lg_ref[...] + lb_ref[...]
        y = _silu(y) * _silu(gate_ref[pl.ds(r0, rc), :].astype(F32))
        o_ref[pl.ds(r0, rc), :] = y.astype(BF16)
        return carry

    lax.fori_loop(0, ts // rc, chunk, 0)
    u_ref[0, 0:CONV_HALO, :] = u_ref[0, ts:ts + CONV_HALO, :]


def _conv_branch(proj, conv_w, conv_b, ln_g, ln_b, batch, seq, ts, rc=16):
    t = batch * seq
    ns = seq // ts
    row = lambda b, s: b * ns + s
    vec = pl.BlockSpec((1, D_BRANCH), lambda b, s: (0, 0))
    return pl.pallas_call(
        functools.partial(_conv_body, ts=ts, rc=rc),
        grid=(batch, ns),
        in_specs=[
            pl.BlockSpec((ts, D_BRANCH), lambda b, s: (row(b, s), COL_A_VAL)),
            pl.BlockSpec((ts, D_BRANCH), lambda b, s: (row(b, s), COL_A_GLU)),
            pl.BlockSpec((ts, D_BRANCH), lambda b, s: (row(b, s), COL_A_GATE)),
            pl.BlockSpec((CONV_K + 1, D_BRANCH), lambda b, s: (0, 0)),
            vec, vec, vec,
        ],
        out_specs=pl.BlockSpec((ts, D_BRANCH), lambda b, s: (row(b, s), 0)),
        out_shape=jax.ShapeDtypeStruct((t, D_BRANCH), BF16),
        scratch_shapes=[pltpu.VMEM((SUBLANES, ts + CONV_HALO, D_BRANCH), F32)],
        compiler_params=_params("parallel", "arbitrary"),
        name="conv_branch",
    )(proj, proj, proj, conv_w, conv_b, ln_g, ln_b)


N_LEVELS = 6
N_EXPO = N_LEVELS + 2


def _hgrn_constants():
    c = HGRN_CHUNK
    t = np.arange(c)[:, None]
    j = np.arange(c)[None, :]
    mats = [(j <= t), (j > t)]
    masks = [np.eye(c, dtype=bool)]
    for lv in range(N_LEVELS):
        m = 1 << lv
        blk = t // (2 * m)
        mid = blk * 2 * m + m
        upper = t >= mid
        e_upper = upper & (j >= mid) & (j <= t)
        e_lower = (~upper) & (j > t) & (j < mid)
        mats.append(e_upper | e_lower)
        tt, ss = np.arange(c)[:, None], np.arange(c)[None, :]
        masks.append((tt // (2 * m) == ss // (2 * m)) & (tt % (2 * m) >= m) & (ss % (2 * m) < m))
    expo = np.concatenate(mats, axis=0).astype(np.float32)
    sel = np.zeros((c, c), np.int32)
    sel[:] = -1
    for i, mk in enumerate(masks):
        sel[mk] = i
    return expo, sel


def _hgrn_body(q_ref, f_ref, i_ref, gate_ref, lb_ref, ng_ref, expo_ref, sel_ref, o_ref, st_ref, *, nchunk):
    c = HGRN_CHUNK

    @pl.when(pl.program_id(2) == 0)
    def _():
        st_ref[...] = jnp.zeros((HGRN_HEAD, HGRN_HEAD), F32)

    lbf = lb_ref[...]
    log_lb = jnp.log(lbf)
    log1m_lb = jnp.log1p(-lbf)
    sel = sel_ref[...]
    nt = (((1,), (1,)), ((), ()))
    tn = (((0,), (0,)), ((), ()))

    def chunk(ci, carry):
        r0 = pl.multiple_of(ci * c, c)
        z = f_ref[pl.ds(r0, c), :].astype(F32)
        qf = _silu(q_ref[pl.ds(r0, c), :].astype(F32))
        v = i_ref[pl.ds(r0, c), :]
        f = lbf + (1.0 - lbf) * _sigmoid(z)
        k = 1.0 - f
        b = log1m_lb + _log_sigmoid(z)
        hi = jnp.maximum(log_lb, b)
        log_f = hi + jnp.log1p(jnp.exp(-jnp.abs(log_lb - b)))
        pieces = _split_bf16(log_f, 2)
        g = jnp.dot(expo_ref[...], jnp.concatenate(pieces, axis=1), preferred_element_type=F32)
        w = jnp.exp(g[:, :HGRN_HEAD] + g[:, HGRN_HEAD:])

        st = st_ref[...]
        q_state = (qf * w[0:c]).astype(BF16)
        k_state = (k * w[c:2 * c]).astype(BF16)
        a = lax.dot_general(qf.astype(BF16), k.astype(BF16), nt, preferred_element_type=F32)
        a = jnp.where(sel == 0, a, 0.0)
        for lv in range(N_LEVELS):
            wl = w[(2 + lv) * c:(3 + lv) * c]
            p = lax.dot_general((qf * wl).astype(BF16), (k * wl).astype(BF16), nt, preferred_element_type=F32)
            a = jnp.where(sel == lv + 1, p, a)
        o = jnp.dot(a.astype(BF16), v, preferred_element_type=F32)
        o = o + lax.dot_general(q_state, st.astype(BF16), nt, preferred_element_type=F32)
        decay = w[c - 1:c]
        st_ref[...] = st * decay + lax.dot_general(v, k_state, tn, preferred_element_type=F32)

        o = o * lax.rsqrt(jnp.mean(o * o, axis=-1, keepdims=True) + EPS) * ng_ref[...]
        o = o * _silu(gate_ref[pl.ds(r0, c), :].astype(F32))
        o_ref[pl.ds(r0, c), :] = o.astype(BF16)
        return carry

    lax.fori_loop(0, nchunk, chunk, 0)


def _hgrn_branch(proj, lb, norm_g, expo, sel, batch, seq, tl):
    t = batch * seq
    nl = seq // tl
    heads = D_BRANCH // HGRN_HEAD
    col = lambda base: (lambda b, h, s: (b * nl + s, base * heads + h))
    vec = pl.BlockSpec((1, HGRN_HEAD), lambda b, h, s: (0, h))
    return pl.pallas_call(
        functools.partial(_hgrn_body, nchunk=tl // HGRN_CHUNK),
        grid=(batch, heads, nl),
        in_specs=[
            pl.BlockSpec((tl, HGRN_HEAD), col(COL_B_Q)),
            pl.BlockSpec((tl, HGRN_HEAD), col(COL_B_F)),
            pl.BlockSpec((tl, HGRN_HEAD), col(COL_B_I)),
            pl.BlockSpec((tl, HGRN_HEAD), col(COL_B_GATE)),
            vec, vec,
            pl.BlockSpec((N_EXPO * HGRN_CHUNK, HGRN_CHUNK), lambda b, h, s: (0, 0)),
            pl.BlockSpec((HGRN_CHUNK, HGRN_CHUNK), lambda b, h, s: (0, 0)),
        ],
        out_specs=pl.BlockSpec((tl, HGRN_HEAD), lambda b, h, s: (b * nl + s, h)),
        out_shape=jax.ShapeDtypeStruct((t, D_BRANCH), BF16),
        scratch_shapes=[pltpu.VMEM((HGRN_HEAD, HGRN_HEAD), F32)],
        compiler_params=_params("parallel", "parallel", "arbitrary"),
        name="hgrn_branch",
    )(proj, proj, proj, proj, lb, norm_g, expo, sel)


def _fox_constants():
    pair = np.zeros((LANES, LANES), np.float32)
    pair[:FOX_HEAD, :FOX_HEAD] = 1.0 / FOX_HEAD
    pair[FOX_HEAD:, FOX_HEAD:] = 1.0 / FOX_HEAD
    wide = FOX_HEADS * LANES
    place_q = np.zeros((LANES, wide), np.float32)
    place_k = np.zeros((LANES, wide), np.float32)
    feat = np.zeros((1, wide), np.float32)
    for h in range(FOX_HEADS):
        lo = h * LANES + (0 if h % 2 == 0 else FOX_HEAD)
        feat[0, lo:lo + FOX_HEAD] = 1.0
        aug = h * LANES + (FOX_HEAD if h % 2 == 0 else 0)
        for p in range(N_SPLIT):
            place_q[p * FOX_HEADS + h, aug + p] = 1.0
            place_q[AUG_ONE_LANE, aug + N_SPLIT + p] = 1.0
            place_k[AUG_ONE_LANE, aug + p] = 1.0
            place_k[p * FOX_HEADS + h, aug + N_SPLIT + p] = -1.0
    return pair, place_q, place_k, feat


def _fox_pre_body(q_ref, k_ref, fl_ref, fb_ref, qg_ref, kg_ref, pair_ref, pq_ref, pk_ref, feat_ref, tri_ref,
                  qo_ref, ko_ref, carry_ref, *, ts):
    @pl.when(pl.program_id(1) == 0)
    def _():
        carry_ref[...] = jnp.zeros((1, LANES), F32)

    ls = _log_sigmoid(fl_ref[...] + fb_ref[...])
    tri = tri_ref[...]
    cum = carry_ref[...]
    fcum = jnp.zeros((ts, LANES), F32)
    for p in _split_bf16(ls, N_SPLIT):
        fcum = fcum + jnp.dot(tri, p, preferred_element_type=F32)
    fcum = fcum + cum
    carry_ref[...] = fcum[ts - 1:ts, :]

    lane = lax.broadcasted_iota(jnp.int32, (ts, LANES), 1)
    p0, p1, p2 = _split_bf16(fcum, N_SPLIT)
    pieces = jnp.where(lane < FOX_HEADS, p0.astype(F32),
                       jnp.where(lane < 2 * FOX_HEADS, p1.astype(F32),
                                 jnp.where(lane < 3 * FOX_HEADS, p2.astype(F32),
                                           jnp.where(lane == AUG_ONE_LANE, 1.0, 0.0)))).astype(BF16)

    scale = 1.0 / math.sqrt(FOX_HEAD)
    for src_ref, g_ref, place_ref, dst_ref, mul in ((q_ref, qg_ref, pq_ref, qo_ref, scale), (k_ref, kg_ref, pk_ref, ko_ref, 1.0)):
        for pr in range(FOX_HEADS // 2):
            sl = slice(pr * LANES, (pr + 1) * LANES)
            x = src_ref[:, sl].astype(F32)
            ms = jnp.dot((x * x).astype(BF16), pair_ref[...], preferred_element_type=F32)
            xn = x * lax.rsqrt(ms + EPS) * (g_ref[...] * mul)
            for h in (2 * pr, 2 * pr + 1):
                hs = slice(h * LANES, (h + 1) * LANES)
                aug = jnp.dot(pieces, place_ref[:, hs], preferred_element_type=F32)
                dst_ref[:, hs] = (xn * feat_ref[:, hs] + aug).astype(BF16)


def _fox_prologue(proj, f_logit, f_bias, qn_g, kn_g, consts, batch, seq, ts):
    t = batch * seq
    ns = seq // ts
    pair, place_q, place_k, feat, tri = consts
    wide = FOX_HEADS * LANES
    row = lambda b, s: b * ns + s
    const = lambda shape: pl.BlockSpec(shape, lambda b, s: (0, 0))
    return pl.pallas_call(
        functools.partial(_fox_pre_body, ts=ts),
        grid=(batch, ns),
        in_specs=[
            pl.BlockSpec((ts, D_BRANCH), lambda b, s: (row(b, s), COL_C_Q)),
            pl.BlockSpec((ts, D_BRANCH), lambda b, s: (row(b, s), COL_C_K)),
            pl.BlockSpec((ts, LANES), lambda b, s: (row(b, s), 0)),
            const((1, LANES)), const((1, LANES)), const((1, LANES)),
            const((LANES, LANES)), const((LANES, wide)), const((LANES, wide)), const((1, wide)),
            const((ts, ts)),
        ],
        out_specs=[
            pl.BlockSpec((ts, wide), lambda b, s: (row(b, s), 0)),
            pl.BlockSpec((ts, wide), lambda b, s: (row(b, s), 0)),
        ],
        out_shape=[jax.ShapeDtypeStruct((t, wide), BF16), jax.ShapeDtypeStruct((t, wide), BF16)],
        scratch_shapes=[pltpu.VMEM((1, LANES), F32)],
        compiler_params=_params("parallel", "arbitrary"),
        name="fox_prologue",
    )(proj, proj, f_logit, f_bias, qn_g, kn_g, pair, place_q, place_k, feat, tri)


def _fox_body(q_ref, k_ref, v_ref, gate_ref, o_ref, m_ref, l_ref, acc_ref, *, tq, tk):
    qi = pl.program_id(2)
    nt = (((1,), (1,)), ((), ()))
    rows = lax.broadcasted_iota(jnp.int32, (tq, tk), 0)
    cols = lax.broadcasted_iota(jnp.int32, (tq, tk), 1)
    causal = cols <= rows

    for hh in range(2):
        hs = slice(hh * LANES, (hh + 1) * LANES)
        m_ref[hh] = jnp.full((tq, LANES), -jnp.inf, F32)
        l_ref[hh] = jnp.zeros((tq, LANES), F32)
        acc_ref[hh] = jnp.zeros((tq, LANES), F32)

        def step(j, masked, hh=hh, hs=hs):
            k0 = pl.multiple_of(j * tk, tk)
            s = lax.dot_general(q_ref[:, hs], k_ref[pl.ds(k0, tk), hs], nt, preferred_element_type=F32)
            if masked:
                s = jnp.where(causal, s, -jnp.inf)
            m_prev = m_ref[hh]
            m_new = jnp.maximum(m_prev, jnp.max(s, axis=-1, keepdims=True))
            alpha = jnp.exp(m_prev - m_new)
            p = jnp.exp(s - m_new[:, 0:1])
            l_ref[hh] = alpha * l_ref[hh] + jnp.sum(p, axis=-1, keepdims=True)
            acc_ref[hh] = alpha * acc_ref[hh] + jnp.dot(p.astype(BF16), v_ref[pl.ds(k0, tk), :],
                                                        preferred_element_type=F32)
            m_ref[hh] = m_new

        def body(j, carry):
            step(j, False)
            return carry

        lax.fori_loop(0, qi, body, 0)
        step(qi, True)

    lane = lax.broadcasted_iota(jnp.int32, (tq, LANES), 1)
    first = lane < FOX_HEAD
    out = jnp.where(first, acc_ref[0] / l_ref[0], acc_ref[1] / l_ref[1])
    o_ref[...] = (out * _silu(gate_ref[...].astype(F32))).astype(BF16)


def _fox_attention(qa, ka, proj, batch, seq, tq):
    t = batch * seq
    nq = seq // tq
    pairs = FOX_HEADS // 2
    return pl.pallas_call(
        functools.partial(_fox_body, tq=tq, tk=tq),
        grid=(batch, pairs, nq),
        in_specs=[
            pl.BlockSpec((tq, 2 * LANES), lambda b, p, i: (b * nq + i, p)),
            pl.BlockSpec((seq, 2 * LANES), lambda b, p, i: (b, p)),
            pl.BlockSpec((seq, LANES), lambda b, p, i: (b, COL_C_V * pairs + p)),
            pl.BlockSpec((tq, LANES), lambda b, p, i: (b * nq + i, COL_C_GATE * pairs + p)),
        ],
        out_specs=pl.BlockSpec((tq, LANES), lambda b, p, i: (b * nq + i, p)),
        out_shape=jax.ShapeDtypeStruct((t, D_BRANCH), BF16),
        scratch_shapes=[pltpu.VMEM((2, tq, LANES), F32)] * 3,
        compiler_params=_params("parallel", "parallel", "arbitrary"),
        name="fox_attention",
    )(qa, ka, proj, proj)


def _merge_body(x_ref, ya_ref, yb_ref, yc_ref, ga_ref, gb_ref, gc_ref, wb_ref, wo_ref, o_ref):
    mixed = None
    for i, (y_ref, g_ref) in enumerate(((ya_ref, ga_ref), (yb_ref, gb_ref), (yc_ref, gc_ref))):
        d = jnp.dot(y_ref[...], wb_ref[i], preferred_element_type=F32)
        term = _sigmoid(g_ref[...].astype(F32)) * d
        mixed = term if mixed is None else mixed + term
    o_ref[...] = x_ref[...] + jnp.dot(mixed.astype(BF16), wo_ref[...], preferred_element_type=F32)


def _merge(x2, ya, yb, yc, proj, w_branch, w_out, tm):
    t = x2.shape[0]
    rows = lambda c: pl.BlockSpec((tm, D_MODEL), lambda m: (m, c))
    return pl.pallas_call(
        _merge_body,
        grid=(t // tm,),
        in_specs=[
            rows(0), rows(0), rows(0), rows(0),
            rows(COL_MERGE), rows(COL_MERGE + 1), rows(COL_MERGE + 2),
            pl.BlockSpec((N_BRANCH, D_BRANCH, D_MODEL), lambda m: (0, 0, 0)),
            pl.BlockSpec((D_MODEL, D_MODEL), lambda m: (0, 0)),
        ],
        out_specs=rows(0),
        out_shape=jax.ShapeDtypeStruct((t, D_MODEL), F32),
        compiler_params=_params("parallel"),
        name="merge_out",
    )(x2, ya, yb, yc, proj, proj, proj, w_branch, w_out)


def _tiles(seq):
    return dict(
        tm_in=min(1024, seq), tn_in=1024,
        ts_conv=min(256, seq),
        tl_hgrn=min(512, seq),
        ts_fox=min(512, seq),
        tq=min(512, seq),
        tm_merge=min(512, seq),
    )


def kernel(x, norm_g, w_in, conv_w, conv_b, conv_ln_g, conv_ln_b, hgrn_lb_logits, hgrn_norm_g, fox_f_bias,
           fox_qn_g, fox_kn_g, w_branch, w_out):
    batch, seq, _ = x.shape
    depth = w_in.shape[0]
    tl = _tiles(seq)
    t = batch * seq

    p = jax.nn.softmax(hgrn_lb_logits.astype(F32), axis=0)
    cum = jnp.cumsum(p, axis=0)
    lower_bounds = cum - cum[0:1]

    expo_np, sel_np = _hgrn_constants()
    expo = jnp.asarray(expo_np, BF16)
    sel = jnp.asarray(sel_np)
    pair_np, pq_np, pk_np, feat_np = _fox_constants()
    ts = tl["ts_fox"]
    tri = jnp.asarray(np.tril(np.ones((ts, ts), np.float32)), BF16)
    fox_consts = (jnp.asarray(pair_np, BF16), jnp.asarray(pq_np, BF16), jnp.asarray(pk_np, BF16),
                  jnp.asarray(feat_np, F32), tri)

    x2 = x.reshape(t, D_MODEL)
    for l in range(depth):
        w = w_in[l]
        w_main = jnp.concatenate([w[:, :C_F_OFFSET], w[:, C_F_OFFSET + FOX_HEADS:]], axis=1).astype(BF16)
        wf = w[:, C_F_OFFSET:C_F_OFFSET + FOX_HEADS]
        w_f = jnp.concatenate([wf] * N_SPLIT + [jnp.zeros((D_MODEL, LANES - N_SPLIT * FOX_HEADS), F32)],
                              axis=1).astype(BF16)
        fb = jnp.concatenate([fox_f_bias[l]] * N_SPLIT + [jnp.zeros((LANES - N_SPLIT * FOX_HEADS,), F32)])[None, :]
        qg = jnp.tile(fox_qn_g[l], 2)[None, :]
        kg = jnp.tile(fox_kn_g[l], 2)[None, :]
        cw = jnp.concatenate([conv_w[l], jnp.zeros((1, D_BRANCH), F32)], axis=0)

        proj, f_logit = _inproj(x2, norm_g[l][None, :], w_main, w_f, tl["tm_in"], tl["tn_in"])
        ya = _conv_branch(proj, cw, conv_b[l][None, :], conv_ln_g[l][None, :], conv_ln_b[l][None, :],
                          batch, seq, tl["ts_conv"])
        yb = _hgrn_branch(proj, lower_bounds[l][None, :], hgrn_norm_g[l][None, :], expo, sel, batch, seq,
                          tl["tl_hgrn"])
        qa, ka = _fox_prologue(proj, f_logit, fb, qg, kg, fox_consts, batch, seq, ts)
        yc = _fox_attention(qa, ka, proj, batch, seq, tl["tq"])
        x2 = _merge(x2, ya, yb, yc, proj, w_branch[l].astype(BF16), w_out[l].astype(BF16), tl["tm_merge"])
    return x2.reshape(batch, seq, D_MODEL)
```

```python
import functools
import math

import jax
import jax.numpy as jnp
import numpy as np
from jax import lax
from jax.experimental import pallas as pl
from jax.experimental.pallas import tpu as pltpu

F32 = jnp.float32
BF16 = jnp.bfloat16

D_MODEL = 1024
D_BRANCH = 1024
N_BRANCH = 3
CONV_K = 31
HGRN_HEAD = 128
HGRN_CHUNK = 64
FOX_HEAD = 64
FOX_HEADS = D_BRANCH // FOX_HEAD
EPS = 1e-6
LOG2E = math.log2(math.e)

LANES = 128
SUBLANES = 8
VMEM_LIMIT = 56 * 1024 * 1024

COL_A_VAL, COL_A_GLU, COL_A_GATE = 0, 1, 2
COL_B_Q, COL_B_F, COL_B_I, COL_B_GATE = 3, 4, 5, 6
COL_C_Q, COL_C_K, COL_C_V, COL_C_GATE = 7, 8, 9, 10
COL_MERGE = 11
N_MAIN = 14 * D_BRANCH
C_F_OFFSET = 11 * D_BRANCH

CONV_HALO = 32
N_SPLIT = 3
AUG_ONE_LANE = 3 * FOX_HEADS


def _sigmoid(x):
    return 0.5 * jnp.tanh(0.5 * x) + 0.5


def _silu(x):
    h = 0.5 * x
    return h * jnp.tanh(h) + h


def _log_sigmoid(x):
    return jnp.minimum(x, 0.0) - jnp.log(1.0 + jnp.exp(-jnp.abs(x)))


def _split_bf16(x, n):
    pieces = []
    r = x
    for _ in range(n):
        p = r.astype(BF16)
        pieces.append(p)
        r = r - p.astype(F32)
    return pieces


def _params(*sem):
    return pltpu.CompilerParams(dimension_semantics=sem, vmem_limit_bytes=VMEM_LIMIT)


def _inproj_body(x_ref, g_ref, w_ref, wf_ref, proj_ref, f_ref, h_ref):
    @pl.when(pl.program_id(1) == 0)
    def _():
        x = x_ref[...]
        ms = jnp.mean(x * x, axis=-1, keepdims=True)
        h = (x * lax.rsqrt(ms + EPS) * g_ref[...]).astype(BF16)
        h_ref[...] = h
        f_ref[...] = jnp.dot(h, wf_ref[...], preferred_element_type=F32)

    proj_ref[...] = jnp.dot(h_ref[...], w_ref[...], preferred_element_type=F32).astype(BF16)


def _inproj(x2, g, w_main, w_f, tm, tn):
    t = x2.shape[0]
    return pl.pallas_call(
        _inproj_body,
        grid=(t // tm, N_MAIN // tn),
        in_specs=[
            pl.BlockSpec((tm, D_MODEL), lambda m, n: (m, 0)),
            pl.BlockSpec((1, D_MODEL), lambda m, n: (0, 0)),
            pl.BlockSpec((D_MODEL, tn), lambda m, n: (0, n)),
            pl.BlockSpec((D_MODEL, LANES), lambda m, n: (0, 0)),
        ],
        out_specs=[
            pl.BlockSpec((tm, tn), lambda m, n: (m, n)),
            pl.BlockSpec((tm, LANES), lambda m, n: (m, 0)),
        ],
        out_shape=[
            jax.ShapeDtypeStruct((t, N_MAIN), BF16),
            jax.ShapeDtypeStruct((t, LANES), F32),
        ],
        scratch_shapes=[pltpu.VMEM((tm, D_MODEL), BF16)],
        compiler_params=_params("parallel", "arbitrary"),
        name="inproj",
    )(x2, g, w_main, w_f)


def _conv_body(val_ref, glu_ref, gate_ref, w_ref, cb_ref, lg_ref, lb_ref, o_ref, u_ref, y_ref, *, ts, rc):
    @pl.when(pl.program_id(1) == 0)
    def _():
        u_ref[0, 0:CONV_HALO, :] = jnp.zeros((CONV_HALO, D_BRANCH), F32)

    u_ref[0, CONV_HALO:CONV_HALO + ts, :] = val_ref[...].astype(F32) * _sigmoid(glu_ref[...].astype(F32))
    blk = 5 * SUBLANES
    for b in range(1, SUBLANES):
        for r in range(SUBLANES, CONV_HALO + ts, blk):
            n = min(blk, CONV_HALO + ts - r)
            u_ref[b, r:r + n, :] = u_ref[0, r - b:r - b + n, :]

    groups = rc // SUBLANES
    nlb = D_BRANCH // LANES

    def chunk(c, carry):
        r0 = pl.multiple_of(c * rc, rc)
        total = jnp.zeros((rc, LANES), F32)
        for lb in range(nlb):
            ls = slice(lb * LANES, (lb + 1) * LANES)
            acc = jnp.broadcast_to(cb_ref[:, ls][None], (groups, SUBLANES, LANES))
            for j in range(CONV_K):
                back = CONV_K - 1 - j
                a, b = divmod(back, SUBLANES)
                u = u_ref[b, pl.ds(r0 + CONV_HALO - SUBLANES * a, rc), ls]
                acc = acc + w_ref[j, :, ls][None] * u.reshape(groups, SUBLANES, LANES)
            acc = acc.reshape(rc, LANES)
            y_ref[:, ls] = acc
            total = total + acc
        mu = jnp.sum(total, axis=-1, keepdims=True) * (1.0 / D_BRANCH)
        sq = jnp.zeros((rc, LANES), F32)
        for lb in range(nlb):
            d = y_ref[:, lb * LANES:(lb + 1) * LANES] - mu
            sq = sq + d * d
        rstd = lax.rsqrt(jnp.sum(sq, axis=-1, keepdims=True) * (1.0 / D_BRANCH) + EPS)
        for lb in range(nlb):
            ls = slice(lb * LANES, (lb + 1) * LANES)
            y = (y_ref[:, ls] - mu) * rstd * lg_ref[:, ls] + lb_ref[:, ls]
            y = _silu(y) * _silu(gate_ref[pl.ds(r0, rc), ls].astype(F32))
            o_ref[pl.ds(r0, rc), ls] = y.astype(BF16)
        return carry

    lax.fori_loop(0, ts // rc, chunk, 0)
    u_ref[0, 0:CONV_HALO, :] = u_ref[0, ts:ts + CONV_HALO, :]


def _conv_branch(proj, conv_w, conv_b, ln_g, ln_b, batch, seq, ts, rc=64):
    t = batch * seq
    ns = seq // ts
    row = lambda b, s: b * ns + s
    vec = pl.BlockSpec((1, D_BRANCH), lambda b, s: (0, 0))
    return pl.pallas_call(
        functools.partial(_conv_body, ts=ts, rc=rc),
        grid=(batch, ns),
        in_specs=[
            pl.BlockSpec((ts, D_BRANCH), lambda b, s: (row(b, s), COL_A_VAL)),
            pl.BlockSpec((ts, D_BRANCH), lambda b, s: (row(b, s), COL_A_GLU)),
            pl.BlockSpec((ts, D_BRANCH), lambda b, s: (row(b, s), COL_A_GATE)),
            pl.BlockSpec((CONV_K, SUBLANES, D_BRANCH), lambda b, s: (0, 0, 0)),
            vec, vec, vec,
        ],
        out_specs=pl.BlockSpec((ts, D_BRANCH), lambda b, s: (row(b, s), 0)),
        out_shape=jax.ShapeDtypeStruct((t, D_BRANCH), BF16),
        scratch_shapes=[pltpu.VMEM((SUBLANES, ts + CONV_HALO, D_BRANCH), F32),
                        pltpu.VMEM((rc, D_BRANCH), F32)],
        compiler_params=_params("parallel", "arbitrary"),
        name="conv_branch",
    )(proj, proj, proj, conv_w, conv_b, ln_g, ln_b)


HGRN_MATMUL_LEVELS = (2, 4)
HGRN_BCAST_LEVELS = (8, 16, 32)
HGRN_PRODUCTS = 7
HGRN_UNROLL = 2


def _hgrn_constants():
    c = HGRN_CHUNK
    t = np.arange(c)[:, None]
    j = np.arange(c)[None, :]
    mats = [(j <= t)]
    masks = [np.eye(c, dtype=bool)]
    for m in (1,) + HGRN_MATMUL_LEVELS + HGRN_BCAST_LEVELS:
        mid = (t // (2 * m)) * 2 * m + m
        upper = t >= mid
        if m in HGRN_MATMUL_LEVELS:
            e_upper = upper & (j >= mid) & (j <= t)
            e_lower = (~upper) & (j > t) & (j < mid)
            mats.append(e_upper | e_lower)
        masks.append((t // (2 * m) == j // (2 * m)) & (t % (2 * m) >= m) & (j % (2 * m) < m))
    expo = np.concatenate(mats, axis=0).astype(np.float32)
    assert len(masks) == HGRN_PRODUCTS
    sel = np.full((c, c), -1, np.int32)
    for i, mk in enumerate(masks):
        sel[mk] = i
    return expo, sel


def _hgrn_body(q_ref, f_ref, i_ref, gate_ref, lb_ref, ng_ref, expo_ref, sel_ref, o_ref, st_ref, c_ref, *,
               nchunk, nh):
    c = HGRN_CHUNK
    dh = HGRN_HEAD
    gw = nh * dh

    @pl.when(pl.program_id(2) == 0)
    def _():
        st_ref[...] = jnp.zeros((nh, dh, dh), F32)

    lbf = lb_ref[...]
    log_lb = jnp.log(lbf)
    log1m_lb = jnp.log(1.0 - lbf)
    sel = sel_ref[...]
    nt = (((1,), (1,)), ((), ()))
    tn = (((0,), (0,)), ((), ()))

    def chunk(ci, slot):
        r0 = pl.multiple_of(ci * c, c)
        z = f_ref[pl.ds(r0, c), :].astype(F32)
        qf = _silu(q_ref[pl.ds(r0, c), :].astype(F32))
        f = lbf + (1.0 - lbf) * _sigmoid(z)
        k = 1.0 - f
        b = log1m_lb + _log_sigmoid(z)
        log_f = jnp.maximum(log_lb, b) + jnp.log(1.0 + jnp.exp(-jnp.abs(log_lb - b)))
        pieces = _split_bf16(log_f * LOG2E, 2)
        g = jnp.dot(expo_ref[...], jnp.concatenate(pieces, axis=1), preferred_element_type=F32)
        g = g[:, :gw] + g[:, gw:]
        c2 = g[0:c]
        c_ref[slot] = c2
        w_read = jnp.exp2(c2)
        w_write = jnp.exp2(c_ref[slot, pl.ds(c - 1, 1), :] - c2)
        wl = [jnp.exp2(g[(1 + i) * c:(2 + i) * c]) for i in range(len(HGRN_MATMUL_LEVELS))]
        for m in HGRN_BCAST_LEVELS:
            segs = [-jnp.abs(c2[s0:s0 + 2 * m] - c_ref[slot, pl.ds(s0 + m - 1, 1), :]) for s0 in range(0, c, 2 * m)]
            wl.append(jnp.exp2(jnp.concatenate(segs, axis=0) if len(segs) > 1 else segs[0]))
        gate = _silu(gate_ref[pl.ds(r0, c), :].astype(F32))
        qb = qf.astype(BF16)
        kb = k.astype(BF16)
        wlb = [x.astype(BF16) for x in wl]
        xs = [qb, qb * f.astype(BF16)] + [qb * x for x in wlb]
        ys = [kb, kb] + [kb * x for x in wlb]
        q_state = qb * w_read.astype(BF16)
        k_state = kb * w_write.astype(BF16)

        for h in range(nh):
            hs = slice(h * dh, (h + 1) * dh)
            v = i_ref[pl.ds(r0, c), hs]
            st = st_ref[h]
            a = None
            for p in range(HGRN_PRODUCTS):
                pp = lax.dot_general(xs[p][:, hs], ys[p][:, hs], nt, preferred_element_type=F32)
                a = jnp.where(sel == p, pp, 0.0 if a is None else a)
            o = jnp.dot(a.astype(BF16), v, preferred_element_type=F32)
            o = o + lax.dot_general(q_state[:, hs], st.astype(BF16), nt, preferred_element_type=F32)
            decay = w_read[c - 1:c, hs]
            st_ref[h] = st * decay + lax.dot_general(v, k_state[:, hs], tn, preferred_element_type=F32)
            o = o * lax.rsqrt(jnp.mean(o * o, axis=-1, keepdims=True) + EPS) * ng_ref[:, hs]
            o_ref[pl.ds(r0, c), hs] = (o * gate[:, hs]).astype(BF16)

    def group(i, carry):
        for u in range(HGRN_UNROLL):
            chunk(i * HGRN_UNROLL + u, u)
        return carry

    lax.fori_loop(0, nchunk // HGRN_UNROLL, group, 0)


def _hgrn_branch(proj, lb, norm_g, expo, sel, batch, seq, tl, nh):
    t = batch * seq
    nl = seq // tl
    gw = nh * HGRN_HEAD
    groups = D_BRANCH // gw
    col = lambda base: (lambda b, h, s: (b * nl + s, base * groups + h))
    vec = pl.BlockSpec((1, gw), lambda b, h, s: (0, h))
    return pl.pallas_call(
        functools.partial(_hgrn_body, nchunk=tl // HGRN_CHUNK, nh=nh),
        grid=(batch, groups, nl),
        in_specs=[
            pl.BlockSpec((tl, gw), col(COL_B_Q)),
            pl.BlockSpec((tl, gw), col(COL_B_F)),
            pl.BlockSpec((tl, gw), col(COL_B_I)),
            pl.BlockSpec((tl, gw), col(COL_B_GATE)),
            vec, vec,
            pl.BlockSpec(((1 + len(HGRN_MATMUL_LEVELS)) * HGRN_CHUNK, HGRN_CHUNK), lambda b, h, s: (0, 0)),
            pl.BlockSpec((HGRN_CHUNK, HGRN_CHUNK), lambda b, h, s: (0, 0)),
        ],
        out_specs=pl.BlockSpec((tl, gw), lambda b, h, s: (b * nl + s, h)),
        out_shape=jax.ShapeDtypeStruct((t, D_BRANCH), BF16),
        scratch_shapes=[pltpu.VMEM((nh, HGRN_HEAD, HGRN_HEAD), F32),
                        pltpu.VMEM((HGRN_UNROLL, HGRN_CHUNK, gw), F32)],
        compiler_params=_params("parallel", "parallel", "arbitrary"),
        name="hgrn_branch",
    )(proj, proj, proj, proj, lb, norm_g, expo, sel)


def _fox_constants():
    pair = np.zeros((LANES, LANES), np.float32)
    pair[:FOX_HEAD, :FOX_HEAD] = 1.0 / FOX_HEAD
    pair[FOX_HEAD:, FOX_HEAD:] = 1.0 / FOX_HEAD
    wide = FOX_HEADS * LANES
    place_q = np.zeros((LANES, wide), np.float32)
    place_k = np.zeros((LANES, wide), np.float32)
    feat = np.zeros((1, wide), np.float32)
    one_v = np.zeros((1, wide), np.float32)
    for h in range(FOX_HEADS):
        lo = h * LANES + (0 if h % 2 == 0 else FOX_HEAD)
        feat[0, lo:lo + FOX_HEAD] = 1.0
        aug = h * LANES + (FOX_HEAD if h % 2 == 0 else 0)
        one_v[0, aug] = 1.0
        for p in range(N_SPLIT):
            place_q[p * FOX_HEADS + h, aug + p] = 1.0
            place_q[AUG_ONE_LANE, aug + N_SPLIT + p] = 1.0
            place_k[AUG_ONE_LANE, aug + p] = 1.0
            place_k[p * FOX_HEADS + h, aug + N_SPLIT + p] = -1.0
    return pair, place_q, place_k, feat, one_v


def _fox_pre_body(q_ref, k_ref, v_ref, fl_ref, fb_ref, qg_ref, kg_ref, pair_ref, pq_ref, pk_ref, feat_ref,
                  onev_ref, tri_ref, qo_ref, ko_ref, vo_ref, carry_ref, *, ts):
    @pl.when(pl.program_id(1) == 0)
    def _():
        carry_ref[...] = jnp.zeros((1, LANES), F32)

    ls = _log_sigmoid(fl_ref[...] + fb_ref[...])
    tri = tri_ref[...]
    fcum = jnp.zeros((ts, LANES), F32)
    for p in _split_bf16(ls, N_SPLIT):
        fcum = fcum + jnp.dot(tri, p, preferred_element_type=F32)
    fcum = fcum + carry_ref[...]
    carry_ref[...] = fcum[ts - 1:ts, :]

    lane = lax.broadcasted_iota(jnp.int32, (ts, LANES), 1)
    p0, p1, p2 = _split_bf16(fcum * LOG2E, N_SPLIT)
    pieces = jnp.where(lane < FOX_HEADS, p0.astype(F32),
                       jnp.where(lane < 2 * FOX_HEADS, p1.astype(F32),
                                 jnp.where(lane < 3 * FOX_HEADS, p2.astype(F32),
                                           jnp.where(lane == AUG_ONE_LANE, 1.0, 0.0)))).astype(BF16)

    scale = LOG2E / math.sqrt(FOX_HEAD)
    for src_ref, g_ref, place_ref, dst_ref, mul in ((q_ref, qg_ref, pq_ref, qo_ref, scale), (k_ref, kg_ref, pk_ref, ko_ref, 1.0)):
        for pr in range(FOX_HEADS // 2):
            sl = slice(pr * LANES, (pr + 1) * LANES)
            x = src_ref[:, sl].astype(F32)
            ms = jnp.dot((x * x).astype(BF16), pair_ref[...], preferred_element_type=F32)
            xn = x * lax.rsqrt(ms + EPS) * (g_ref[...] * mul)
            for h in (2 * pr, 2 * pr + 1):
                hs = slice(h * LANES, (h + 1) * LANES)
                aug = jnp.dot(pieces, place_ref[:, hs], preferred_element_type=F32)
                dst_ref[:, hs] = (xn * feat_ref[:, hs] + aug).astype(BF16)
    for pr in range(FOX_HEADS // 2):
        v = v_ref[:, pr * LANES:(pr + 1) * LANES].astype(F32)
        for h in (2 * pr, 2 * pr + 1):
            hs = slice(h * LANES, (h + 1) * LANES)
            vo_ref[:, hs] = (v * feat_ref[:, hs] + onev_ref[:, hs]).astype(BF16)


def _fox_prologue(proj, f_logit, f_bias, qn_g, kn_g, consts, batch, seq, ts):
    t = batch * seq
    ns = seq // ts
    pair, place_q, place_k, feat, one_v, tri = consts
    wide = FOX_HEADS * LANES
    row = lambda b, s: b * ns + s
    const = lambda shape: pl.BlockSpec(shape, lambda b, s: (0, 0))
    wide_out = pl.BlockSpec((ts, wide), lambda b, s: (row(b, s), 0))
    return pl.pallas_call(
        functools.partial(_fox_pre_body, ts=ts),
        grid=(batch, ns),
        in_specs=[
            pl.BlockSpec((ts, D_BRANCH), lambda b, s: (row(b, s), COL_C_Q)),
            pl.BlockSpec((ts, D_BRANCH), lambda b, s: (row(b, s), COL_C_K)),
            pl.BlockSpec((ts, D_BRANCH), lambda b, s: (row(b, s), COL_C_V)),
            pl.BlockSpec((ts, LANES), lambda b, s: (row(b, s), 0)),
            const((1, LANES)), const((1, LANES)), const((1, LANES)),
            const((LANES, LANES)), const((LANES, wide)), const((LANES, wide)), const((1, wide)), const((1, wide)),
            const((ts, ts)),
        ],
        out_specs=[wide_out, wide_out, wide_out],
        out_shape=[jax.ShapeDtypeStruct((t, wide), BF16)] * 3,
        scratch_shapes=[pltpu.VMEM((1, LANES), F32)],
        compiler_params=_params("parallel", "arbitrary"),
        name="fox_prologue",
    )(proj, proj, proj, f_logit, f_bias, qn_g, kn_g, pair, place_q, place_k, feat, one_v, tri)


def _fox_body(q_ref, k_ref, v_ref, gate_ref, o_ref, m_ref, acc_ref, *, tq, tk):
    qi = pl.program_id(2)
    nt = (((1,), (1,)), ((), ()))
    ncb = tk // LANES

    for hh in range(2):
        m_ref[hh] = jnp.full((tq, LANES), -jnp.inf, F32)
        acc_ref[hh] = jnp.zeros((tq, LANES), F32)

    def step(j, masked):
        k0 = pl.multiple_of(j * tk, tk)
        for hh in range(2):
            hs = slice(hh * LANES, (hh + 1) * LANES)
            s = lax.dot_general(q_ref[:, hs], k_ref[pl.ds(k0, tk), hs], nt, preferred_element_type=F32)
            if masked:
                rows = lax.broadcasted_iota(jnp.int32, (tq, tk), 0)
                cols = lax.broadcasted_iota(jnp.int32, (tq, tk), 1)
                s = jnp.where(cols <= rows, s, -jnp.inf)
            blocks = [s[:, cb * LANES:(cb + 1) * LANES] for cb in range(ncb)]
            mx = blocks[0]
            for blk in blocks[1:]:
                mx = jnp.maximum(mx, blk)
            m_prev = m_ref[hh]
            m_new = jnp.maximum(m_prev, jnp.max(mx, axis=-1, keepdims=True))
            alpha = jnp.exp2(m_prev - m_new)
            p = jnp.concatenate([jnp.exp2(blk - m_new).astype(BF16) for blk in blocks], axis=1)
            acc_ref[hh] = alpha * acc_ref[hh] + jnp.dot(p, v_ref[pl.ds(k0, tk), hs], preferred_element_type=F32)
            m_ref[hh] = m_new

    def body(jj, carry):
        step(2 * jj, False)
        step(2 * jj + 1, False)
        return carry

    lax.fori_loop(0, qi // 2, body, 0)

    @pl.when(qi % 2 == 1)
    def _():
        step(qi - 1, False)

    step(qi, True)

    lane = lax.broadcasted_iota(jnp.int32, (tq, LANES), 1)
    acc0 = acc_ref[0]
    acc1 = acc_ref[1]
    out = jnp.where(lane < FOX_HEAD, acc0 / acc0[:, FOX_HEAD:FOX_HEAD + 1], acc1 / acc1[:, 0:1])
    o_ref[...] = (out * _silu(gate_ref[...].astype(F32))).astype(BF16)


def _fox_attention(qa, ka, va, proj, batch, seq, tq):
    t = batch * seq
    nq = seq // tq
    pairs = FOX_HEADS // 2
    kv = pl.BlockSpec((seq, 2 * LANES), lambda b, p, i: (b, p))
    return pl.pallas_call(
        functools.partial(_fox_body, tq=tq, tk=tq),
        grid=(batch, pairs, nq),
        in_specs=[
            pl.BlockSpec((tq, 2 * LANES), lambda b, p, i: (b * nq + i, p)),
            kv, kv,
            pl.BlockSpec((tq, LANES), lambda b, p, i: (b * nq + i, COL_C_GATE * pairs + p)),
        ],
        out_specs=pl.BlockSpec((tq, LANES), lambda b, p, i: (b * nq + i, p)),
        out_shape=jax.ShapeDtypeStruct((t, D_BRANCH), BF16),
        scratch_shapes=[pltpu.VMEM((2, tq, LANES), F32)] * 2,
        compiler_params=_params("parallel", "parallel", "arbitrary"),
        name="fox_attention",
    )(qa, ka, va, proj)


def _merge_body(x_ref, ya_ref, yb_ref, yc_ref, ga_ref, gb_ref, gc_ref, wb_ref, wo_ref, o_ref):
    mixed = None
    for i, (y_ref, g_ref) in enumerate(((ya_ref, ga_ref), (yb_ref, gb_ref), (yc_ref, gc_ref))):
        d = jnp.dot(y_ref[...], wb_ref[i], preferred_element_type=F32)
        term = _sigmoid(g_ref[...].astype(F32)) * d
        mixed = term if mixed is None else mixed + term
    o_ref[...] = x_ref[...] + jnp.dot(mixed.astype(BF16), wo_ref[...], preferred_element_type=F32)


def _merge(x2, ya, yb, yc, proj, w_branch, w_out, tm):
    t = x2.shape[0]
    rows = lambda c: pl.BlockSpec((tm, D_MODEL), lambda m: (m, c))
    return pl.pallas_call(
        _merge_body,
        grid=(t // tm,),
        in_specs=[
            rows(0), rows(0), rows(0), rows(0),
            rows(COL_MERGE), rows(COL_MERGE + 1), rows(COL_MERGE + 2),
            pl.BlockSpec((N_BRANCH, D_BRANCH, D_MODEL), lambda m: (0, 0, 0)),
            pl.BlockSpec((D_MODEL, D_MODEL), lambda m: (0, 0)),
        ],
        out_specs=rows(0),
        out_shape=jax.ShapeDtypeStruct((t, D_MODEL), F32),
        compiler_params=_params("parallel"),
        name="merge_out",
    )(x2, ya, yb, yc, proj, proj, proj, w_branch, w_out)


def _tiles(seq):
    return dict(
        tm_in=min(1024, seq), tn_in=1024,
        ts_conv=min(256, seq),
        tl_hgrn=min(512, seq), nh_hgrn=4,
        ts_fox=min(512, seq),
        tq=min(512, seq),
        tm_merge=min(512, seq),
    )


def kernel(x, norm_g, w_in, conv_w, conv_b, conv_ln_g, conv_ln_b, hgrn_lb_logits, hgrn_norm_g, fox_f_bias,
           fox_qn_g, fox_kn_g, w_branch, w_out):
    batch, seq, _ = x.shape
    depth = w_in.shape[0]
    tl = _tiles(seq)
    t = batch * seq

    p = jax.nn.softmax(hgrn_lb_logits.astype(F32), axis=0)
    cum = jnp.cumsum(p, axis=0)
    lower_bounds = cum - cum[0:1]

    expo_np, sel_np = _hgrn_constants()
    expo = jnp.asarray(expo_np, BF16)
    sel = jnp.asarray(sel_np)
    pair_np, pq_np, pk_np, feat_np, onev_np = _fox_constants()
    ts = tl["ts_fox"]
    tri = jnp.asarray(np.tril(np.ones((ts, ts), np.float32)), BF16)
    fox_consts = (jnp.asarray(pair_np, BF16), jnp.asarray(pq_np, BF16), jnp.asarray(pk_np, BF16),
                  jnp.asarray(feat_np, F32), jnp.asarray(onev_np, F32), tri)

    x2 = x.reshape(t, D_MODEL)
    for l in range(depth):
        w = w_in[l]
        w_main = jnp.concatenate([w[:, :C_F_OFFSET], w[:, C_F_OFFSET + FOX_HEADS:]], axis=1).astype(BF16)
        wf = w[:, C_F_OFFSET:C_F_OFFSET + FOX_HEADS]
        w_f = jnp.concatenate([wf] * N_SPLIT + [jnp.zeros((D_MODEL, LANES - N_SPLIT * FOX_HEADS), F32)],
                              axis=1).astype(BF16)
        fb = jnp.concatenate([fox_f_bias[l]] * N_SPLIT + [jnp.zeros((LANES - N_SPLIT * FOX_HEADS,), F32)])[None, :]
        qg = jnp.tile(fox_qn_g[l], 2)[None, :]
        kg = jnp.tile(fox_kn_g[l], 2)[None, :]
        cw = jnp.broadcast_to(conv_w[l][:, None, :], (CONV_K, SUBLANES, D_BRANCH))

        proj, f_logit = _inproj(x2, norm_g[l][None, :], w_main, w_f, tl["tm_in"], tl["tn_in"])
        ya = _conv_branch(proj, cw, conv_b[l][None, :], conv_ln_g[l][None, :], conv_ln_b[l][None, :],
                          batch, seq, tl["ts_conv"])
        yb = _hgrn_branch(proj, lower_bounds[l][None, :], hgrn_norm_g[l][None, :], expo, sel, batch, seq,
                          tl["tl_hgrn"], tl["nh_hgrn"])
        qa, ka, va = _fox_prologue(proj, f_logit, fb, qg, kg, fox_consts, batch, seq, ts)
        yc = _fox_attention(qa, ka, va, proj, batch, seq, tl["tq"])
        x2 = _merge(x2, ya, yb, yc, proj, w_branch[l].astype(BF16), w_out[l].astype(BF16), tl["tm_merge"])
    return x2.reshape(batch, seq, D_MODEL)
```

```python
import functools
import math

import jax
import jax.numpy as jnp
import numpy as np
from jax import lax
from jax.experimental import pallas as pl
from jax.experimental.pallas import tpu as pltpu

F32 = jnp.float32
BF16 = jnp.bfloat16

D_MODEL = 1024
D_BRANCH = 1024
N_BRANCH = 3
CONV_K = 31
HGRN_HEAD = 128
HGRN_CHUNK = 64
FOX_HEAD = 64
FOX_HEADS = D_BRANCH // FOX_HEAD
EPS = 1e-6
LOG2E = math.log2(math.e)

LANES = 128
SUBLANES = 8
VMEM_LIMIT = 56 * 1024 * 1024

COL_A_VAL, COL_A_GLU, COL_A_GATE = 0, 1, 2
COL_B_Q, COL_B_F, COL_B_I, COL_B_GATE = 3, 4, 5, 6
COL_C_Q, COL_C_K, COL_C_V, COL_C_GATE = 7, 8, 9, 10
COL_MERGE = 11
N_MAIN = 14 * D_BRANCH
C_F_OFFSET = 11 * D_BRANCH

CONV_HALO = 32
N_SPLIT = 3
AUG_ONE_LANE = 3 * FOX_HEADS


def _sigmoid(x):
    return 0.5 * jnp.tanh(0.5 * x) + 0.5


def _silu(x):
    h = 0.5 * x
    return h * jnp.tanh(h) + h


def _log_sigmoid(x):
    return jnp.minimum(x, 0.0) - jnp.log(1.0 + jnp.exp(-jnp.abs(x)))


def _split_bf16(x, n):
    pieces = []
    r = x
    for _ in range(n):
        p = r.astype(BF16)
        pieces.append(p)
        r = r - p.astype(F32)
    return pieces


def _params(*sem):
    return pltpu.CompilerParams(dimension_semantics=sem, vmem_limit_bytes=VMEM_LIMIT)


def _inproj_body(x_ref, g_ref, w_ref, wf_ref, proj_ref, f_ref, h_ref):
    @pl.when(pl.program_id(1) == 0)
    def _():
        x = x_ref[...]
        ms = jnp.mean(x * x, axis=-1, keepdims=True)
        h = (x * lax.rsqrt(ms + EPS) * g_ref[...]).astype(BF16)
        h_ref[...] = h
        f_ref[...] = jnp.dot(h, wf_ref[...], preferred_element_type=F32)

    proj_ref[...] = jnp.dot(h_ref[...], w_ref[...], preferred_element_type=F32).astype(BF16)


def _inproj(x2, g, w_main, w_f, tm, tn):
    t = x2.shape[0]
    return pl.pallas_call(
        _inproj_body,
        grid=(t // tm, N_MAIN // tn),
        in_specs=[
            pl.BlockSpec((tm, D_MODEL), lambda m, n: (m, 0)),
            pl.BlockSpec((1, D_MODEL), lambda m, n: (0, 0)),
            pl.BlockSpec((D_MODEL, tn), lambda m, n: (0, n)),
            pl.BlockSpec((D_MODEL, LANES), lambda m, n: (0, 0)),
        ],
        out_specs=[
            pl.BlockSpec((tm, tn), lambda m, n: (m, n)),
            pl.BlockSpec((tm, LANES), lambda m, n: (m, 0)),
        ],
        out_shape=[
            jax.ShapeDtypeStruct((t, N_MAIN), BF16),
            jax.ShapeDtypeStruct((t, LANES), F32),
        ],
        scratch_shapes=[pltpu.VMEM((tm, D_MODEL), BF16)],
        compiler_params=_params("parallel", "arbitrary"),
        name="inproj",
    )(x2, g, w_main, w_f)


def _conv_body(val_ref, glu_ref, gate_ref, w_ref, cb_ref, lg_ref, lb_ref, o_ref, u_ref, y_ref, *, ts, rc, nr):
    @pl.when(pl.program_id(1) == 0)
    def _():
        u_ref[0, 0:CONV_HALO, :] = jnp.zeros((CONV_HALO, D_BRANCH), F32)

    u_ref[0, CONV_HALO:CONV_HALO + ts, :] = val_ref[...].astype(F32) * _sigmoid(glu_ref[...].astype(F32))
    blk = 5 * SUBLANES
    for b in range(1, SUBLANES):
        for r in range(SUBLANES, CONV_HALO + ts, blk):
            n = min(blk, CONV_HALO + ts - r)
            u_ref[b, r:r + n, :] = u_ref[0, r - b:r - b + n, :]

    groups = rc // SUBLANES
    nlb = D_BRANCH // LANES
    back_groups = (CONV_K - 1) // SUBLANES

    for lb in range(nlb):
        ls = slice(lb * LANES, (lb + 1) * LANES)
        taps = [w_ref[j, :, ls][None] for j in range(CONV_K)]
        bias = jnp.broadcast_to(cb_ref[:, ls][None], (groups, SUBLANES, LANES))

        def rows(c, carry, ls=ls, taps=taps, bias=bias):
            r0 = pl.multiple_of(c * rc, rc)
            acc = bias
            for b in range(SUBLANES):
                win = u_ref[b, pl.ds(r0 + CONV_HALO - back_groups * SUBLANES, rc + back_groups * SUBLANES), ls]
                for a in range(back_groups + 1):
                    j = CONV_K - 1 - (SUBLANES * a + b)
                    if j < 0:
                        continue
                    lo = (back_groups - a) * SUBLANES
                    acc = acc + taps[j] * win[lo:lo + rc].reshape(groups, SUBLANES, LANES)
            y_ref[pl.ds(r0, rc), ls] = acc.reshape(rc, LANES)
            return carry

        lax.fori_loop(0, ts // rc, rows, 0)

    def norm(c, carry):
        r0 = pl.multiple_of(c * nr, nr)
        y = y_ref[pl.ds(r0, nr), :]
        mu = jnp.mean(y, axis=-1, keepdims=True)
        d = y - mu
        var = jnp.mean(d * d, axis=-1, keepdims=True)
        y = d * lax.rsqrt(var + EPS) * lg_ref[...] + lb_ref[...]
        y = _silu(y) * _silu(gate_ref[pl.ds(r0, nr), :].astype(F32))
        o_ref[pl.ds(r0, nr), :] = y.astype(BF16)
        return carry

    lax.fori_loop(0, ts // nr, norm, 0)
    u_ref[0, 0:CONV_HALO, :] = u_ref[0, ts:ts + CONV_HALO, :]


def _conv_branch(proj, conv_w, conv_b, ln_g, ln_b, batch, seq, ts, rc=64, nr=64):
    t = batch * seq
    ns = seq // ts
    row = lambda b, s: b * ns + s
    vec = pl.BlockSpec((1, D_BRANCH), lambda b, s: (0, 0))
    return pl.pallas_call(
        functools.partial(_conv_body, ts=ts, rc=rc, nr=nr),
        grid=(batch, ns),
        in_specs=[
            pl.BlockSpec((ts, D_BRANCH), lambda b, s: (row(b, s), COL_A_VAL)),
            pl.BlockSpec((ts, D_BRANCH), lambda b, s: (row(b, s), COL_A_GLU)),
            pl.BlockSpec((ts, D_BRANCH), lambda b, s: (row(b, s), COL_A_GATE)),
            pl.BlockSpec((CONV_K, SUBLANES, D_BRANCH), lambda b, s: (0, 0, 0)),
            vec, vec, vec,
        ],
        out_specs=pl.BlockSpec((ts, D_BRANCH), lambda b, s: (row(b, s), 0)),
        out_shape=jax.ShapeDtypeStruct((t, D_BRANCH), BF16),
        scratch_shapes=[pltpu.VMEM((SUBLANES, ts + CONV_HALO, D_BRANCH), F32),
                        pltpu.VMEM((ts, D_BRANCH), F32)],
        compiler_params=_params("parallel", "arbitrary"),
        name="conv_branch",
    )(proj, proj, proj, conv_w, conv_b, ln_g, ln_b)


HGRN_MATMUL_LEVELS = (2, 4)
HGRN_BCAST_LEVELS = (8, 16, 32)
HGRN_PRODUCTS = 7
HGRN_UNROLL = 4
HGRN_DIRECT_SPAN = 100.0


def _hgrn_constants():
    c = HGRN_CHUNK
    t = np.arange(c)[:, None]
    j = np.arange(c)[None, :]
    mats = [(j <= t)]
    masks = [np.eye(c, dtype=bool)]
    for m in (1,) + HGRN_MATMUL_LEVELS + HGRN_BCAST_LEVELS:
        mid = (t // (2 * m)) * 2 * m + m
        upper = t >= mid
        if m in HGRN_MATMUL_LEVELS:
            e_upper = upper & (j >= mid) & (j <= t)
            e_lower = (~upper) & (j > t) & (j < mid)
            mats.append(e_upper | e_lower)
        masks.append((t // (2 * m) == j // (2 * m)) & (t % (2 * m) >= m) & (j % (2 * m) < m))
    expo = np.concatenate(mats, axis=0).astype(np.float32)
    assert len(masks) == HGRN_PRODUCTS
    sel = np.full((c, c), -1, np.int32)
    for i, mk in enumerate(masks):
        sel[mk] = i
    return expo, sel


def _hgrn_body(q_ref, f_ref, i_ref, gate_ref, lb_ref, ng_ref, expo_ref, sel_ref, o_ref, st_ref, c_ref, *,
               nchunk, nh):
    c = HGRN_CHUNK
    dh = HGRN_HEAD
    gw = nh * dh

    @pl.when(pl.program_id(2) == 0)
    def _():
        st_ref[...] = jnp.zeros((nh, dh, dh), F32)

    lbf = lb_ref[...]
    log_lb = jnp.log(lbf)
    log1m_lb = jnp.log(1.0 - lbf)
    sel = sel_ref[...]
    nt = (((1,), (1,)), ((), ()))
    tn = (((0,), (0,)), ((), ()))

    def prepare(ci, slot):
        r0 = pl.multiple_of(ci * c, c)
        z = f_ref[pl.ds(r0, c), :].astype(F32)
        qf = _silu(q_ref[pl.ds(r0, c), :].astype(F32))
        f = lbf + (1.0 - lbf) * _sigmoid(z)
        k = 1.0 - f
        b = log1m_lb + _log_sigmoid(z)
        log_f = jnp.maximum(log_lb, b) + jnp.log(1.0 + jnp.exp(-jnp.abs(log_lb - b)))
        pieces = jnp.concatenate(_split_bf16(log_f * LOG2E, 2), axis=1)
        g = jnp.dot(expo_ref[0:c, :], pieces, preferred_element_type=F32)
        c2 = g[:, :gw] + g[:, gw:]
        c_ref[slot] = c2
        c_last = c_ref[slot, pl.ds(c - 1, 1), :]
        w_read = jnp.exp2(c2)
        qb = qf.astype(BF16)
        kb = k.astype(BF16)
        return dict(
            r0=r0, slot=slot, c2=c2, c_last=c_last, qb=qb, kb=kb, fb=f.astype(BF16), pieces=pieces,
            decay=w_read[c - 1:c, :],
            q_state=qb * w_read.astype(BF16),
            k_state=kb * jnp.exp2(c_last - c2).astype(BF16),
            gate=_silu(gate_ref[pl.ds(r0, c), :].astype(F32)))

    def finish(pre, scores):
        r0 = pre["r0"]
        for h in range(nh):
            hs = slice(h * dh, (h + 1) * dh)
            v = i_ref[pl.ds(r0, c), hs]
            st = st_ref[h]
            o = jnp.dot(scores(hs).astype(BF16), v, preferred_element_type=F32)
            o = o + lax.dot_general(pre["q_state"][:, hs], st.astype(BF16), nt, preferred_element_type=F32)
            st_ref[h] = st * pre["decay"][:, hs] + lax.dot_general(v, pre["k_state"][:, hs], tn,
                                                                   preferred_element_type=F32)
            o = o * lax.rsqrt(jnp.mean(o * o, axis=-1, keepdims=True) + EPS) * ng_ref[:, hs]
            o_ref[pl.ds(r0, c), hs] = (o * pre["gate"][:, hs]).astype(BF16)

    def chunk_direct(pre):
        y = pre["kb"] * jnp.exp2(-pre["c2"]).astype(BF16)

        def scores(hs):
            pp = lax.dot_general(pre["q_state"][:, hs], y[:, hs], nt, preferred_element_type=F32)
            return jnp.where(sel >= 0, pp, 0.0)

        finish(pre, scores)

    def chunk_dyadic(pre):
        c2, qb, kb, slot = pre["c2"], pre["qb"], pre["kb"], pre["slot"]
        g = jnp.dot(expo_ref[c:, :], pre["pieces"], preferred_element_type=F32)
        g = g[:, :gw] + g[:, gw:]
        wl = [jnp.exp2(g[i * c:(i + 1) * c]) for i in range(len(HGRN_MATMUL_LEVELS))]
        for m in HGRN_BCAST_LEVELS:
            segs = [-jnp.abs(c2[s0:s0 + 2 * m] - c_ref[slot, pl.ds(s0 + m - 1, 1), :]) for s0 in range(0, c, 2 * m)]
            wl.append(jnp.exp2(jnp.concatenate(segs, axis=0) if len(segs) > 1 else segs[0]))
        wlb = [x.astype(BF16) for x in wl]
        xs = [qb, qb * pre["fb"]] + [qb * x for x in wlb]
        ys = [kb, kb] + [kb * x for x in wlb]

        def scores(hs):
            a = None
            for p in range(HGRN_PRODUCTS):
                pp = lax.dot_general(xs[p][:, hs], ys[p][:, hs], nt, preferred_element_type=F32)
                a = jnp.where(sel == p, pp, 0.0 if a is None else a)
            return a

        finish(pre, scores)

    def group(i, carry):
        pres = [prepare(i * HGRN_UNROLL + u, u) for u in range(HGRN_UNROLL)]
        span = pres[0]["c_last"]
        for pre in pres[1:]:
            span = jnp.minimum(span, pre["c_last"])
        small = jnp.min(span) >= -HGRN_DIRECT_SPAN

        @pl.when(small)
        def _():
            for pre in pres:
                chunk_direct(pre)

        @pl.when(jnp.logical_not(small))
        def _():
            for pre in pres:
                chunk_dyadic(pre)

        return carry

    lax.fori_loop(0, nchunk // HGRN_UNROLL, group, 0)


def _hgrn_branch(proj, lb, norm_g, expo, sel, batch, seq, tl, nh):
    t = batch * seq
    nl = seq // tl
    gw = nh * HGRN_HEAD
    groups = D_BRANCH // gw
    col = lambda base: (lambda b, h, s: (b * nl + s, base * groups + h))
    vec = pl.BlockSpec((1, gw), lambda b, h, s: (0, h))
    return pl.pallas_call(
        functools.partial(_hgrn_body, nchunk=tl // HGRN_CHUNK, nh=nh),
        grid=(batch, groups, nl),
        in_specs=[
            pl.BlockSpec((tl, gw), col(COL_B_Q)),
            pl.BlockSpec((tl, gw), col(COL_B_F)),
            pl.BlockSpec((tl, gw), col(COL_B_I)),
            pl.BlockSpec((tl, gw), col(COL_B_GATE)),
            vec, vec,
            pl.BlockSpec(((1 + len(HGRN_MATMUL_LEVELS)) * HGRN_CHUNK, HGRN_CHUNK), lambda b, h, s: (0, 0)),
            pl.BlockSpec((HGRN_CHUNK, HGRN_CHUNK), lambda b, h, s: (0, 0)),
        ],
        out_specs=pl.BlockSpec((tl, gw), lambda b, h, s: (b * nl + s, h)),
        out_shape=jax.ShapeDtypeStruct((t, D_BRANCH), BF16),
        scratch_shapes=[pltpu.VMEM((nh, HGRN_HEAD, HGRN_HEAD), F32),
                        pltpu.VMEM((HGRN_UNROLL, HGRN_CHUNK, gw), F32)],
        compiler_params=_params("parallel", "parallel", "arbitrary"),
        name="hgrn_branch",
    )(proj, proj, proj, proj, lb, norm_g, expo, sel)


def _fox_constants():
    pair = np.zeros((LANES, LANES), np.float32)
    pair[:FOX_HEAD, :FOX_HEAD] = 1.0 / FOX_HEAD
    pair[FOX_HEAD:, FOX_HEAD:] = 1.0 / FOX_HEAD
    wide = FOX_HEADS * LANES
    place_q = np.zeros((LANES, wide), np.float32)
    place_k = np.zeros((LANES, wide), np.float32)
    feat = np.zeros((1, wide), np.float32)
    one_v = np.zeros((1, wide), np.float32)
    for h in range(FOX_HEADS):
        lo = h * LANES + (0 if h % 2 == 0 else FOX_HEAD)
        feat[0, lo:lo + FOX_HEAD] = 1.0
        aug = h * LANES + (FOX_HEAD if h % 2 == 0 else 0)
        one_v[0, aug] = 1.0
        for p in range(N_SPLIT):
            place_q[p * FOX_HEADS + h, aug + p] = 1.0
            place_q[AUG_ONE_LANE, aug + N_SPLIT + p] = 1.0
            place_k[AUG_ONE_LANE, aug + p] = 1.0
            place_k[p * FOX_HEADS + h, aug + N_SPLIT + p] = -1.0
    return pair, place_q, place_k, feat, one_v


def _fox_pre_body(q_ref, k_ref, v_ref, fl_ref, fb_ref, qg_ref, kg_ref, pair_ref, pq_ref, pk_ref, feat_ref,
                  onev_ref, tri_ref, qo_ref, ko_ref, vo_ref, carry_ref, *, ts):
    @pl.when(pl.program_id(1) == 0)
    def _():
        carry_ref[...] = jnp.zeros((1, LANES), F32)

    ls = _log_sigmoid(fl_ref[...] + fb_ref[...])
    tri = tri_ref[...]
    fcum = jnp.zeros((ts, LANES), F32)
    for p in _split_bf16(ls, N_SPLIT):
        fcum = fcum + jnp.dot(tri, p, preferred_element_type=F32)
    fcum = fcum + carry_ref[...]
    carry_ref[...] = fcum[ts - 1:ts, :]

    lane = lax.broadcasted_iota(jnp.int32, (ts, LANES), 1)
    p0, p1, p2 = _split_bf16(fcum * LOG2E, N_SPLIT)
    pieces = jnp.where(lane < FOX_HEADS, p0.astype(F32),
                       jnp.where(lane < 2 * FOX_HEADS, p1.astype(F32),
                                 jnp.where(lane < 3 * FOX_HEADS, p2.astype(F32),
                                           jnp.where(lane == AUG_ONE_LANE, 1.0, 0.0)))).astype(BF16)

    scale = LOG2E / math.sqrt(FOX_HEAD)
    for src_ref, g_ref, place_ref, dst_ref, mul in ((q_ref, qg_ref, pq_ref, qo_ref, scale), (k_ref, kg_ref, pk_ref, ko_ref, 1.0)):
        for pr in range(FOX_HEADS // 2):
            sl = slice(pr * LANES, (pr + 1) * LANES)
            x = src_ref[:, sl].astype(F32)
            ms = jnp.dot((x * x).astype(BF16), pair_ref[...], preferred_element_type=F32)
            xn = x * lax.rsqrt(ms + EPS) * (g_ref[...] * mul)
            for h in (2 * pr, 2 * pr + 1):
                hs = slice(h * LANES, (h + 1) * LANES)
                aug = jnp.dot(pieces, place_ref[:, hs], preferred_element_type=F32)
                dst_ref[:, hs] = (xn * feat_ref[:, hs] + aug).astype(BF16)
    for pr in range(FOX_HEADS // 2):
        v = v_ref[:, pr * LANES:(pr + 1) * LANES].astype(F32)
        for h in (2 * pr, 2 * pr + 1):
            hs = slice(h * LANES, (h + 1) * LANES)
            vo_ref[:, hs] = (v * feat_ref[:, hs] + onev_ref[:, hs]).astype(BF16)


def _fox_prologue(proj, f_logit, f_bias, qn_g, kn_g, consts, batch, seq, ts):
    t = batch * seq
    ns = seq // ts
    pair, place_q, place_k, feat, one_v, tri = consts
    wide = FOX_HEADS * LANES
    row = lambda b, s: b * ns + s
    const = lambda shape: pl.BlockSpec(shape, lambda b, s: (0, 0))
    wide_out = pl.BlockSpec((ts, wide), lambda b, s: (row(b, s), 0))
    return pl.pallas_call(
        functools.partial(_fox_pre_body, ts=ts),
        grid=(batch, ns),
        in_specs=[
            pl.BlockSpec((ts, D_BRANCH), lambda b, s: (row(b, s), COL_C_Q)),
            pl.BlockSpec((ts, D_BRANCH), lambda b, s: (row(b, s), COL_C_K)),
            pl.BlockSpec((ts, D_BRANCH), lambda b, s: (row(b, s), COL_C_V)),
            pl.BlockSpec((ts, LANES), lambda b, s: (row(b, s), 0)),
            const((1, LANES)), const((1, LANES)), const((1, LANES)),
            const((LANES, LANES)), const((LANES, wide)), const((LANES, wide)), const((1, wide)), const((1, wide)),
            const((ts, ts)),
        ],
        out_specs=[wide_out, wide_out, wide_out],
        out_shape=[jax.ShapeDtypeStruct((t, wide), BF16)] * 3,
        scratch_shapes=[pltpu.VMEM((1, LANES), F32)],
        compiler_params=_params("parallel", "arbitrary"),
        name="fox_prologue",
    )(proj, proj, proj, f_logit, f_bias, qn_g, kn_g, pair, place_q, place_k, feat, one_v, tri)


def _fox_body(q_ref, k_ref, v_ref, gate_ref, o_ref, m_ref, acc_ref, *, tq, tk):
    qi = pl.program_id(2)
    nt = (((1,), (1,)), ((), ()))
    ncb = tk // LANES

    for hh in range(2):
        m_ref[hh] = jnp.full((tq, LANES), -jnp.inf, F32)
        acc_ref[hh] = jnp.zeros((tq, LANES), F32)

    def step(j, r0, nr, masked):
        k0 = pl.multiple_of(j * tk, tk)
        rs = slice(r0, r0 + nr)
        for hh in range(2):
            hs = slice(hh * LANES, (hh + 1) * LANES)
            s = lax.dot_general(q_ref[rs, hs], k_ref[pl.ds(k0, tk), hs], nt, preferred_element_type=F32)
            if masked:
                rows = lax.broadcasted_iota(jnp.int32, (nr, tk), 0)
                cols = lax.broadcasted_iota(jnp.int32, (nr, tk), 1)
                s = jnp.where(cols <= rows, s, -jnp.inf)
            blocks = [s[:, cb * LANES:(cb + 1) * LANES] for cb in range(ncb)]
            mx = blocks[0]
            for blk in blocks[1:]:
                mx = jnp.maximum(mx, blk)
            m_prev = m_ref[hh, rs]
            m_new = jnp.maximum(m_prev, jnp.max(mx, axis=-1, keepdims=True))
            alpha = jnp.exp2(m_prev - m_new)
            p = jnp.concatenate([jnp.exp2(blk - m_new).astype(BF16) for blk in blocks], axis=1)
            acc_ref[hh, rs] = alpha * acc_ref[hh, rs] + jnp.dot(p, v_ref[pl.ds(k0, tk), hs],
                                                                preferred_element_type=F32)
            m_ref[hh, rs] = m_new

    nsub = tq // tk

    def body(jj, carry):
        step(2 * jj, 0, tq, False)
        step(2 * jj + 1, 0, tq, False)
        return carry

    nfull = qi * nsub
    lax.fori_loop(0, nfull // 2, body, 0)
    if nsub % 2 == 1:
        @pl.when(nfull % 2 == 1)
        def _():
            step(nfull - 1, 0, tq, False)

    for a in range(nsub):
        for b in range(a + 1):
            step(nfull + b, a * tk, tk, b == a)

    lane = lax.broadcasted_iota(jnp.int32, (tq, LANES), 1)
    acc0 = acc_ref[0]
    acc1 = acc_ref[1]
    out = jnp.where(lane < FOX_HEAD, acc0 / acc0[:, FOX_HEAD:FOX_HEAD + 1], acc1 / acc1[:, 0:1])
    o_ref[...] = (out * _silu(gate_ref[...].astype(F32))).astype(BF16)


def _fox_attention(qa, ka, va, proj, batch, seq, tq, tk):
    t = batch * seq
    nq = seq // tq
    pairs = FOX_HEADS // 2
    kv = pl.BlockSpec((seq, 2 * LANES), lambda b, p, i: (b, p))
    return pl.pallas_call(
        functools.partial(_fox_body, tq=tq, tk=tk),
        grid=(batch, pairs, nq),
        in_specs=[
            pl.BlockSpec((tq, 2 * LANES), lambda b, p, i: (b * nq + i, p)),
            kv, kv,
            pl.BlockSpec((tq, LANES), lambda b, p, i: (b * nq + i, COL_C_GATE * pairs + p)),
        ],
        out_specs=pl.BlockSpec((tq, LANES), lambda b, p, i: (b * nq + i, p)),
        out_shape=jax.ShapeDtypeStruct((t, D_BRANCH), BF16),
        scratch_shapes=[pltpu.VMEM((2, tq, LANES), F32)] * 2,
        compiler_params=_params("parallel", "parallel", "arbitrary"),
        name="fox_attention",
    )(qa, ka, va, proj)


def _merge_body(x_ref, ya_ref, yb_ref, yc_ref, ga_ref, gb_ref, gc_ref, wb_ref, wo_ref, o_ref):
    mixed = None
    for i, (y_ref, g_ref) in enumerate(((ya_ref, ga_ref), (yb_ref, gb_ref), (yc_ref, gc_ref))):
        d = jnp.dot(y_ref[...], wb_ref[i], preferred_element_type=F32)
        term = _sigmoid(g_ref[...].astype(F32)) * d
        mixed = term if mixed is None else mixed + term
    o_ref[...] = x_ref[...] + jnp.dot(mixed.astype(BF16), wo_ref[...], preferred_element_type=F32)


def _merge(x2, ya, yb, yc, proj, w_branch, w_out, tm):
    t = x2.shape[0]
    rows = lambda c: pl.BlockSpec((tm, D_MODEL), lambda m: (m, c))
    return pl.pallas_call(
        _merge_body,
        grid=(t // tm,),
        in_specs=[
            rows(0), rows(0), rows(0), rows(0),
            rows(COL_MERGE), rows(COL_MERGE + 1), rows(COL_MERGE + 2),
            pl.BlockSpec((N_BRANCH, D_BRANCH, D_MODEL), lambda m: (0, 0, 0)),
            pl.BlockSpec((D_MODEL, D_MODEL), lambda m: (0, 0)),
        ],
        out_specs=rows(0),
        out_shape=jax.ShapeDtypeStruct((t, D_MODEL), F32),
        compiler_params=_params("parallel"),
        name="merge_out",
    )(x2, ya, yb, yc, proj, proj, proj, w_branch, w_out)


def _tiles(seq):
    return dict(
        tm_in=min(2048, seq), tn_in=1024,
        ts_conv=min(256, seq),
        tl_hgrn=min(512, seq), nh_hgrn=4,
        ts_fox=min(512, seq),
        tq=min(1024, seq), tk=min(512, seq),
        tm_merge=min(512, seq),
    )


def kernel(x, norm_g, w_in, conv_w, conv_b, conv_ln_g, conv_ln_b, hgrn_lb_logits, hgrn_norm_g, fox_f_bias,
           fox_qn_g, fox_kn_g, w_branch, w_out):
    batch, seq, _ = x.shape
    depth = w_in.shape[0]
    tl = _tiles(seq)
    t = batch * seq

    p = jax.nn.softmax(hgrn_lb_logits.astype(F32), axis=0)
    cum = jnp.cumsum(p, axis=0)
    lower_bounds = cum - cum[0:1]

    expo_np, sel_np = _hgrn_constants()
    expo = jnp.asarray(expo_np, BF16)
    sel = jnp.asarray(sel_np)
    pair_np, pq_np, pk_np, feat_np, onev_np = _fox_constants()
    ts = tl["ts_fox"]
    tri = jnp.asarray(np.tril(np.ones((ts, ts), np.float32)), BF16)
    fox_consts = (jnp.asarray(pair_np, BF16), jnp.asarray(pq_np, BF16), jnp.asarray(pk_np, BF16),
                  jnp.asarray(feat_np, F32), jnp.asarray(onev_np, F32), tri)

    x2 = x.reshape(t, D_MODEL)
    for l in range(depth):
        w = w_in[l]
        w_main = jnp.concatenate([w[:, :C_F_OFFSET], w[:, C_F_OFFSET + FOX_HEADS:]], axis=1).astype(BF16)
        wf = w[:, C_F_OFFSET:C_F_OFFSET + FOX_HEADS]
        w_f = jnp.concatenate([wf] * N_SPLIT + [jnp.zeros((D_MODEL, LANES - N_SPLIT * FOX_HEADS), F32)],
                              axis=1).astype(BF16)
        fb = jnp.concatenate([fox_f_bias[l]] * N_SPLIT + [jnp.zeros((LANES - N_SPLIT * FOX_HEADS,), F32)])[None, :]
        qg = jnp.tile(fox_qn_g[l], 2)[None, :]
        kg = jnp.tile(fox_kn_g[l], 2)[None, :]
        cw = jnp.broadcast_to(conv_w[l][:, None, :], (CONV_K, SUBLANES, D_BRANCH))

        proj, f_logit = _inproj(x2, norm_g[l][None, :], w_main, w_f, tl["tm_in"], tl["tn_in"])
        ya = _conv_branch(proj, cw, conv_b[l][None, :], conv_ln_g[l][None, :], conv_ln_b[l][None, :],
                          batch, seq, tl["ts_conv"])
        yb = _hgrn_branch(proj, lower_bounds[l][None, :], hgrn_norm_g[l][None, :], expo, sel, batch, seq,
                          tl["tl_hgrn"], tl["nh_hgrn"])
        qa, ka, va = _fox_prologue(proj, f_logit, fb, qg, kg, fox_consts, batch, seq, ts)
        yc = _fox_attention(qa, ka, va, proj, batch, seq, tl["tq"], tl["tk"])
        x2 = _merge(x2, ya, yb, yc, proj, w_branch[l].astype(BF16), w_out[l].astype(BF16), tl["tm_merge"])
    return x2.reshape(batch, seq, D_MODEL)
```

```python
import functools
import math

import jax
import jax.numpy as jnp
import numpy as np
from jax import lax
from jax.experimental import pallas as pl
from jax.experimental.pallas import tpu as pltpu

F32 = jnp.float32
BF16 = jnp.bfloat16

D_MODEL = 1024
D_BRANCH = 1024
N_BRANCH = 3
CONV_K = 31
HGRN_HEAD = 128
HGRN_CHUNK = 64
FOX_HEAD = 64
FOX_HEADS = D_BRANCH // FOX_HEAD
EPS = 1e-6
LOG2E = math.log2(math.e)

LANES = 128
SUBLANES = 8
VMEM_LIMIT = 56 * 1024 * 1024

COL_A_VAL, COL_A_GLU, COL_A_GATE = 0, 1, 2
COL_B_Q, COL_B_F, COL_B_I, COL_B_GATE = 3, 4, 5, 6
COL_C_Q, COL_C_K, COL_C_V, COL_C_GATE = 7, 8, 9, 10
COL_MERGE = 11
N_MAIN = 14 * D_BRANCH
C_F_OFFSET = 11 * D_BRANCH

INPROJ_PIECE = 512
CONV_HALO = 32
N_SPLIT = 3
AUG_ONE_LANE = 3 * FOX_HEADS


def _sigmoid(x):
    return 0.5 * jnp.tanh(0.5 * x) + 0.5


def _silu(x):
    h = 0.5 * x
    return h * jnp.tanh(h) + h


def _log2(x):
    return LOG2E * jnp.log(x)


def _log2_sigmoid(x):
    return LOG2E * (jnp.minimum(x, 0.0) - jnp.log(1.0 + jnp.exp2(-LOG2E * jnp.abs(x))))


def _split_bf16(x, n):
    pieces = []
    r = x
    for _ in range(n):
        p = r.astype(BF16)
        pieces.append(p)
        r = r - p.astype(F32)
    return pieces


def _params(*sem):
    return pltpu.CompilerParams(dimension_semantics=sem, vmem_limit_bytes=VMEM_LIMIT)


def _inproj_body(x_ref, g_ref, w_ref, wf_ref, proj_ref, f_ref, h_ref):
    @pl.when(pl.program_id(1) == 0)
    def _():
        x = x_ref[...]
        ms = jnp.mean(x * x, axis=-1, keepdims=True)
        h = (x * lax.rsqrt(ms + EPS) * g_ref[...]).astype(BF16)
        h_ref[...] = h
        f_ref[...] = jnp.dot(h, wf_ref[...], preferred_element_type=F32)

    for c0 in range(0, proj_ref.shape[1], INPROJ_PIECE):
        cs = slice(c0, c0 + INPROJ_PIECE)
        proj_ref[:, cs] = jnp.dot(h_ref[...], w_ref[:, cs], preferred_element_type=F32).astype(BF16)


def _inproj(x2, g, w_main, w_f, tm, tn):
    t = x2.shape[0]
    return pl.pallas_call(
        _inproj_body,
        grid=(t // tm, N_MAIN // tn),
        in_specs=[
            pl.BlockSpec((tm, D_MODEL), lambda m, n: (m, 0)),
            pl.BlockSpec((1, D_MODEL), lambda m, n: (0, 0)),
            pl.BlockSpec((D_MODEL, tn), lambda m, n: (0, n)),
            pl.BlockSpec((D_MODEL, LANES), lambda m, n: (0, 0)),
        ],
        out_specs=[
            pl.BlockSpec((tm, tn), lambda m, n: (m, n)),
            pl.BlockSpec((tm, LANES), lambda m, n: (m, 0)),
        ],
        out_shape=[
            jax.ShapeDtypeStruct((t, N_MAIN), BF16),
            jax.ShapeDtypeStruct((t, LANES), F32),
        ],
        scratch_shapes=[pltpu.VMEM((tm, D_MODEL), BF16)],
        compiler_params=_params("parallel", "arbitrary"),
        name="inproj",
    )(x2, g, w_main, w_f)


def _conv_body(val_ref, glu_ref, gate_ref, w_ref, cb_ref, lg_ref, lb_ref, o_ref, u_ref, y_ref, *, ts, rc, nr):
    @pl.when(pl.program_id(1) == 0)
    def _():
        u_ref[0, 0:CONV_HALO, :] = jnp.zeros((CONV_HALO, D_BRANCH), F32)

    u_ref[0, CONV_HALO:CONV_HALO + ts, :] = val_ref[...].astype(F32) * _sigmoid(glu_ref[...].astype(F32))
    blk = 5 * SUBLANES
    for b in range(1, SUBLANES):
        for r in range(SUBLANES, CONV_HALO + ts, blk):
            n = min(blk, CONV_HALO + ts - r)
            u_ref[b, r:r + n, :] = u_ref[0, r - b:r - b + n, :]

    groups = rc // SUBLANES
    nlb = D_BRANCH // LANES
    back_groups = (CONV_K - 1) // SUBLANES

    for lb in range(nlb):
        ls = slice(lb * LANES, (lb + 1) * LANES)
        taps = [w_ref[j, :, ls][None] for j in range(CONV_K)]
        bias = jnp.broadcast_to(cb_ref[:, ls][None], (groups, SUBLANES, LANES))

        def rows(c, carry, ls=ls, taps=taps, bias=bias):
            r0 = pl.multiple_of(c * rc, rc)
            accs = [bias, None]
            for b in range(SUBLANES):
                win = u_ref[b, pl.ds(r0 + CONV_HALO - back_groups * SUBLANES, rc + back_groups * SUBLANES), ls]
                for a in range(back_groups + 1):
                    j = CONV_K - 1 - (SUBLANES * a + b)
                    if j < 0:
                        continue
                    lo = (back_groups - a) * SUBLANES
                    term = taps[j] * win[lo:lo + rc].reshape(groups, SUBLANES, LANES)
                    accs[b % 2] = term if accs[b % 2] is None else accs[b % 2] + term
            y_ref[pl.ds(r0, rc), ls] = (accs[0] + accs[1]).reshape(rc, LANES)
            return carry

        lax.fori_loop(0, ts // rc, rows, 0)

    def norm(c, carry):
        r0 = pl.multiple_of(c * nr, nr)
        y = y_ref[pl.ds(r0, nr), :]
        mu = jnp.mean(y, axis=-1, keepdims=True)
        d = y - mu
        var = jnp.mean(d * d, axis=-1, keepdims=True)
        y = d * lax.rsqrt(var + EPS) * lg_ref[...] + lb_ref[...]
        y = _silu(y) * _silu(gate_ref[pl.ds(r0, nr), :].astype(F32))
        o_ref[pl.ds(r0, nr), :] = y.astype(BF16)
        return carry

    lax.fori_loop(0, ts // nr, norm, 0)
    u_ref[0, 0:CONV_HALO, :] = u_ref[0, ts:ts + CONV_HALO, :]


def _conv_branch(proj, conv_w, conv_b, ln_g, ln_b, batch, seq, ts, rc=64, nr=64):
    t = batch * seq
    ns = seq // ts
    row = lambda b, s: b * ns + s
    vec = pl.BlockSpec((1, D_BRANCH), lambda b, s: (0, 0))
    return pl.pallas_call(
        functools.partial(_conv_body, ts=ts, rc=rc, nr=nr),
        grid=(batch, ns),
        in_specs=[
            pl.BlockSpec((ts, D_BRANCH), lambda b, s: (row(b, s), COL_A_VAL)),
            pl.BlockSpec((ts, D_BRANCH), lambda b, s: (row(b, s), COL_A_GLU)),
            pl.BlockSpec((ts, D_BRANCH), lambda b, s: (row(b, s), COL_A_GATE)),
            pl.BlockSpec((CONV_K, SUBLANES, D_BRANCH), lambda b, s: (0, 0, 0)),
            vec, vec, vec,
        ],
        out_specs=pl.BlockSpec((ts, D_BRANCH), lambda b, s: (row(b, s), 0)),
        out_shape=jax.ShapeDtypeStruct((t, D_BRANCH), BF16),
        scratch_shapes=[pltpu.VMEM((SUBLANES, ts + CONV_HALO, D_BRANCH), F32),
                        pltpu.VMEM((ts, D_BRANCH), F32)],
        compiler_params=_params("parallel", "arbitrary"),
        name="conv_branch",
    )(proj, proj, proj, conv_w, conv_b, ln_g, ln_b)


HGRN_MATMUL_LEVELS = (2, 4)
HGRN_BCAST_LEVELS = (8, 16, 32)
HGRN_PRODUCTS = 7
HGRN_UNROLL = 8
HGRN_DIRECT_SPAN = 100.0


def _hgrn_constants():
    c = HGRN_CHUNK
    t = np.arange(c)[:, None]
    j = np.arange(c)[None, :]
    mats = [(j <= t)]
    masks = [np.eye(c, dtype=bool)]
    for m in (1,) + HGRN_MATMUL_LEVELS + HGRN_BCAST_LEVELS:
        mid = (t // (2 * m)) * 2 * m + m
        upper = t >= mid
        if m in HGRN_MATMUL_LEVELS:
            e_upper = upper & (j >= mid) & (j <= t)
            e_lower = (~upper) & (j > t) & (j < mid)
            mats.append(e_upper | e_lower)
        masks.append((t // (2 * m) == j // (2 * m)) & (t % (2 * m) >= m) & (j % (2 * m) < m))
    expo = np.concatenate(mats, axis=0).astype(np.float32)
    assert len(masks) == HGRN_PRODUCTS
    sel = np.full((c, c), -1, np.int32)
    for i, mk in enumerate(masks):
        sel[mk] = i
    return expo, sel


def _hgrn_body(q_ref, f_ref, i_ref, gate_ref, lb_ref, ng_ref, expo_ref, sel_ref, o_ref, st_ref, c_ref, *,
               nchunk, nh):
    c = HGRN_CHUNK
    dh = HGRN_HEAD
    gw = nh * dh

    @pl.when(pl.program_id(2) == 0)
    def _():
        st_ref[...] = jnp.zeros((nh, dh, dh), F32)

    lbf = lb_ref[...]
    log_lb = _log2(lbf)
    log1m_lb = _log2(1.0 - lbf)
    sel = sel_ref[...]
    nt = (((1,), (1,)), ((), ()))
    tn = (((0,), (0,)), ((), ()))

    def prepare(ci, slot):
        r0 = pl.multiple_of(ci * c, c)
        z = f_ref[pl.ds(r0, c), :].astype(F32)
        qf = _silu(q_ref[pl.ds(r0, c), :].astype(F32))
        f = lbf + (1.0 - lbf) * _sigmoid(z)
        k = 1.0 - f
        b = log1m_lb + _log2_sigmoid(z)
        log2_f = jnp.maximum(log_lb, b) + _log2(1.0 + jnp.exp2(-jnp.abs(log_lb - b)))
        pieces = jnp.concatenate(_split_bf16(log2_f, 2), axis=1)
        g = jnp.dot(expo_ref[0:c, :], pieces, preferred_element_type=F32)
        c2 = g[:, :gw] + g[:, gw:]
        c_ref[slot] = c2
        c_last = c_ref[slot, pl.ds(c - 1, 1), :]
        w_read = jnp.exp2(c2)
        qb = qf.astype(BF16)
        kb = k.astype(BF16)
        return dict(
            r0=r0, slot=slot, c2=c2, c_last=c_last, qb=qb, kb=kb, fb=f.astype(BF16), pieces=pieces,
            decay=w_read[c - 1:c, :],
            q_state=qb * w_read.astype(BF16),
            k_state=kb * jnp.exp2(c_last - c2).astype(BF16),
            gate=_silu(gate_ref[pl.ds(r0, c), :].astype(F32)))

    def finish(pre, scores):
        r0 = pre["r0"]
        for h in range(nh):
            hs = slice(h * dh, (h + 1) * dh)
            v = i_ref[pl.ds(r0, c), hs]
            st = st_ref[h]
            o = jnp.dot(scores(hs).astype(BF16), v, preferred_element_type=F32)
            o = o + lax.dot_general(pre["q_state"][:, hs], st.astype(BF16), nt, preferred_element_type=F32)
            st_ref[h] = st * pre["decay"][:, hs] + lax.dot_general(v, pre["k_state"][:, hs], tn,
                                                                   preferred_element_type=F32)
            o = o * lax.rsqrt(jnp.mean(o * o, axis=-1, keepdims=True) + EPS) * ng_ref[:, hs]
            o_ref[pl.ds(r0, c), hs] = (o * pre["gate"][:, hs]).astype(BF16)

    def chunk_direct(pre):
        y = pre["kb"] * jnp.exp2(-pre["c2"]).astype(BF16)

        def scores(hs):
            pp = lax.dot_general(pre["q_state"][:, hs], y[:, hs], nt, preferred_element_type=F32)
            return jnp.where(sel >= 0, pp, 0.0)

        finish(pre, scores)

    def chunk_dyadic(pre):
        c2, qb, kb, slot = pre["c2"], pre["qb"], pre["kb"], pre["slot"]
        g = jnp.dot(expo_ref[c:, :], pre["pieces"], preferred_element_type=F32)
        g = g[:, :gw] + g[:, gw:]
        wl = [jnp.exp2(g[i * c:(i + 1) * c]) for i in range(len(HGRN_MATMUL_LEVELS))]
        for m in HGRN_BCAST_LEVELS:
            segs = [-jnp.abs(c2[s0:s0 + 2 * m] - c_ref[slot, pl.ds(s0 + m - 1, 1), :]) for s0 in range(0, c, 2 * m)]
            wl.append(jnp.exp2(jnp.concatenate(segs, axis=0) if len(segs) > 1 else segs[0]))
        wlb = [x.astype(BF16) for x in wl]
        xs = [qb, qb * pre["fb"]] + [qb * x for x in wlb]
        ys = [kb, kb] + [kb * x for x in wlb]

        def scores(hs):
            a = None
            for p in range(HGRN_PRODUCTS):
                pp = lax.dot_general(xs[p][:, hs], ys[p][:, hs], nt, preferred_element_type=F32)
                a = jnp.where(sel == p, pp, 0.0 if a is None else a)
            return a

        finish(pre, scores)

    def group(i, carry):
        pres = [prepare(i * HGRN_UNROLL + u, u) for u in range(HGRN_UNROLL)]
        span = pres[0]["c_last"]
        for pre in pres[1:]:
            span = jnp.minimum(span, pre["c_last"])
        small = jnp.min(span) >= -HGRN_DIRECT_SPAN

        @pl.when(small)
        def _():
            for pre in pres:
                chunk_direct(pre)

        @pl.when(jnp.logical_not(small))
        def _():
            for pre in pres:
                chunk_dyadic(pre)

        return carry

    lax.fori_loop(0, nchunk // HGRN_UNROLL, group, 0)


def _hgrn_branch(proj, lb, norm_g, expo, sel, batch, seq, tl, nh):
    t = batch * seq
    nl = seq // tl
    gw = nh * HGRN_HEAD
    groups = D_BRANCH // gw
    col = lambda base: (lambda b, h, s: (b * nl + s, base * groups + h))
    vec = pl.BlockSpec((1, gw), lambda b, h, s: (0, h))
    return pl.pallas_call(
        functools.partial(_hgrn_body, nchunk=tl // HGRN_CHUNK, nh=nh),
        grid=(batch, groups, nl),
        in_specs=[
            pl.BlockSpec((tl, gw), col(COL_B_Q)),
            pl.BlockSpec((tl, gw), col(COL_B_F)),
            pl.BlockSpec((tl, gw), col(COL_B_I)),
            pl.BlockSpec((tl, gw), col(COL_B_GATE)),
            vec, vec,
            pl.BlockSpec(((1 + len(HGRN_MATMUL_LEVELS)) * HGRN_CHUNK, HGRN_CHUNK), lambda b, h, s: (0, 0)),
            pl.BlockSpec((HGRN_CHUNK, HGRN_CHUNK), lambda b, h, s: (0, 0)),
        ],
        out_specs=pl.BlockSpec((tl, gw), lambda b, h, s: (b * nl + s, h)),
        out_shape=jax.ShapeDtypeStruct((t, D_BRANCH), BF16),
        scratch_shapes=[pltpu.VMEM((nh, HGRN_HEAD, HGRN_HEAD), F32),
                        pltpu.VMEM((HGRN_UNROLL, HGRN_CHUNK, gw), F32)],
        compiler_params=_params("parallel", "parallel", "arbitrary"),
        name="hgrn_branch",
    )(proj, proj, proj, proj, lb, norm_g, expo, sel)


def _fox_constants():
    pair = np.zeros((LANES, LANES), np.float32)
    pair[:FOX_HEAD, :FOX_HEAD] = 1.0 / FOX_HEAD
    pair[FOX_HEAD:, FOX_HEAD:] = 1.0 / FOX_HEAD
    wide = FOX_HEADS * LANES
    place_q = np.zeros((LANES, wide), np.float32)
    place_k = np.zeros((LANES, wide), np.float32)
    feat = np.zeros((1, wide), np.float32)
    one_v = np.zeros((1, wide), np.float32)
    for h in range(FOX_HEADS):
        lo = h * LANES + (0 if h % 2 == 0 else FOX_HEAD)
        feat[0, lo:lo + FOX_HEAD] = 1.0
        aug = h * LANES + (FOX_HEAD if h % 2 == 0 else 0)
        one_v[0, aug] = 1.0
        for p in range(N_SPLIT):
            place_q[p * FOX_HEADS + h, aug + p] = 1.0
            place_q[AUG_ONE_LANE, aug + N_SPLIT + p] = 1.0
            place_k[AUG_ONE_LANE, aug + p] = 1.0
            place_k[p * FOX_HEADS + h, aug + N_SPLIT + p] = -1.0
    return pair, place_q, place_k, feat, one_v


def _fox_pre_body(q_ref, k_ref, v_ref, fl_ref, fb_ref, qg_ref, kg_ref, pair_ref, pq_ref, pk_ref, feat_ref,
                  onev_ref, tri_ref, qo_ref, ko_ref, vo_ref, carry_ref, *, ts):
    @pl.when(pl.program_id(1) == 0)
    def _():
        carry_ref[...] = jnp.zeros((1, LANES), F32)

    ls = _log2_sigmoid(fl_ref[...] + fb_ref[...])
    tri = tri_ref[...]
    fcum = jnp.zeros((ts, LANES), F32)
    for p in _split_bf16(ls, N_SPLIT):
        fcum = fcum + jnp.dot(tri, p, preferred_element_type=F32)
    fcum = fcum + carry_ref[...]
    carry_ref[...] = fcum[ts - 1:ts, :]

    lane = lax.broadcasted_iota(jnp.int32, (ts, LANES), 1)
    p0, p1, p2 = _split_bf16(fcum, N_SPLIT)
    pieces = jnp.where(lane < FOX_HEADS, p0.astype(F32),
                       jnp.where(lane < 2 * FOX_HEADS, p1.astype(F32),
                                 jnp.where(lane < 3 * FOX_HEADS, p2.astype(F32),
                                           jnp.where(lane == AUG_ONE_LANE, 1.0, 0.0)))).astype(BF16)

    scale = LOG2E / math.sqrt(FOX_HEAD)
    for src_ref, g_ref, place_ref, dst_ref, mul in ((q_ref, qg_ref, pq_ref, qo_ref, scale), (k_ref, kg_ref, pk_ref, ko_ref, 1.0)):
        for pr in range(FOX_HEADS // 2):
            sl = slice(pr * LANES, (pr + 1) * LANES)
            x = src_ref[:, sl].astype(F32)
            ms = jnp.dot((x * x).astype(BF16), pair_ref[...], preferred_element_type=F32)
            xn = x * lax.rsqrt(ms + EPS) * (g_ref[...] * mul)
            for h in (2 * pr, 2 * pr + 1):
                hs = slice(h * LANES, (h + 1) * LANES)
                aug = jnp.dot(pieces, place_ref[:, hs], preferred_element_type=F32)
                dst_ref[:, hs] = (xn * feat_ref[:, hs] + aug).astype(BF16)
    for pr in range(FOX_HEADS // 2):
        v = v_ref[:, pr * LANES:(pr + 1) * LANES].astype(F32)
        for h in (2 * pr, 2 * pr + 1):
            hs = slice(h * LANES, (h + 1) * LANES)
            vo_ref[:, hs] = (v * feat_ref[:, hs] + onev_ref[:, hs]).astype(BF16)


def _fox_prologue(proj, f_logit, f_bias, qn_g, kn_g, consts, batch, seq, ts):
    t = batch * seq
    ns = seq // ts
    pair, place_q, place_k, feat, one_v, tri = consts
    wide = FOX_HEADS * LANES
    row = lambda b, s: b * ns + s
    const = lambda shape: pl.BlockSpec(shape, lambda b, s: (0, 0))
    wide_out = pl.BlockSpec((ts, wide), lambda b, s: (row(b, s), 0))
    return pl.pallas_call(
        functools.partial(_fox_pre_body, ts=ts),
        grid=(batch, ns),
        in_specs=[
            pl.BlockSpec((ts, D_BRANCH), lambda b, s: (row(b, s), COL_C_Q)),
            pl.BlockSpec((ts, D_BRANCH), lambda b, s: (row(b, s), COL_C_K)),
            pl.BlockSpec((ts, D_BRANCH), lambda b, s: (row(b, s), COL_C_V)),
            pl.BlockSpec((ts, LANES), lambda b, s: (row(b, s), 0)),
            const((1, LANES)), const((1, LANES)), const((1, LANES)),
            const((LANES, LANES)), const((LANES, wide)), const((LANES, wide)), const((1, wide)), const((1, wide)),
            const((ts, ts)),
        ],
        out_specs=[wide_out, wide_out, wide_out],
        out_shape=[jax.ShapeDtypeStruct((t, wide), BF16)] * 3,
        scratch_shapes=[pltpu.VMEM((1, LANES), F32)],
        compiler_params=_params("parallel", "arbitrary"),
        name="fox_prologue",
    )(proj, proj, proj, f_logit, f_bias, qn_g, kn_g, pair, place_q, place_k, feat, one_v, tri)


def _fox_body(q_ref, k_ref, v_ref, gate_ref, o_ref, m_ref, acc_ref, *, tq, tk):
    qi = pl.program_id(2)
    nt = (((1,), (1,)), ((), ()))

    for hh in range(2):
        m_ref[hh] = jnp.full((tq, LANES), -jnp.inf, F32)
        acc_ref[hh] = jnp.zeros((tq, LANES), F32)

    def step(j, r0, nr, nkb=1, visible=None):
        k0 = pl.multiple_of(j * tk, tk)
        kw = nkb * tk
        rs = slice(r0, r0 + nr)
        for hh in range(2):
            hs = slice(hh * LANES, (hh + 1) * LANES)
            s = lax.dot_general(q_ref[rs, hs], k_ref[pl.ds(k0, kw), hs], nt, preferred_element_type=F32)
            if visible is not None:
                rows = lax.broadcasted_iota(jnp.int32, (nr, kw), 0)
                cols = lax.broadcasted_iota(jnp.int32, (nr, kw), 1)
                s = jnp.where(cols <= rows + visible, s, -jnp.inf)
            blocks = [s[:, cb * LANES:(cb + 1) * LANES] for cb in range(kw // LANES)]
            mx = blocks[0]
            for blk in blocks[1:]:
                mx = jnp.maximum(mx, blk)
            m_prev = m_ref[hh, rs]
            m_new = jnp.maximum(m_prev, jnp.max(mx, axis=-1, keepdims=True))
            alpha = jnp.exp2(m_prev - m_new)
            p = jnp.concatenate([jnp.exp2(blk - m_new).astype(BF16) for blk in blocks], axis=1)
            acc_ref[hh, rs] = alpha * acc_ref[hh, rs] + jnp.dot(p, v_ref[pl.ds(k0, kw), hs],
                                                                preferred_element_type=F32)
            m_ref[hh, rs] = m_new

    nsub = tq // tk

    def body(jj, carry):
        step(2 * jj, 0, tq)
        step(2 * jj + 1, 0, tq)
        return carry

    nfull = qi * nsub
    lax.fori_loop(0, nfull // 2, body, 0)
    if nsub % 2 == 1:
        @pl.when(nfull % 2 == 1)
        def _():
            step(nfull - 1, 0, tq)

    for a in range(nsub):
        step(nfull, a * tk, tk, nkb=a + 1, visible=a * tk)

    lane = lax.broadcasted_iota(jnp.int32, (tq, LANES), 1)
    acc0 = acc_ref[0]
    acc1 = acc_ref[1]
    out = jnp.where(lane < FOX_HEAD, acc0 / acc0[:, FOX_HEAD:FOX_HEAD + 1], acc1 / acc1[:, 0:1])
    o_ref[...] = (out * _silu(gate_ref[...].astype(F32))).astype(BF16)


def _fox_attention(qa, ka, va, proj, batch, seq, tq, tk):
    t = batch * seq
    nq = seq // tq
    pairs = FOX_HEADS // 2
    kv = pl.BlockSpec((seq, 2 * LANES), lambda b, p, i: (b, p))
    return pl.pallas_call(
        functools.partial(_fox_body, tq=tq, tk=tk),
        grid=(batch, pairs, nq),
        in_specs=[
            pl.BlockSpec((tq, 2 * LANES), lambda b, p, i: (b * nq + i, p)),
            kv, kv,
            pl.BlockSpec((tq, LANES), lambda b, p, i: (b * nq + i, COL_C_GATE * pairs + p)),
        ],
        out_specs=pl.BlockSpec((tq, LANES), lambda b, p, i: (b * nq + i, p)),
        out_shape=jax.ShapeDtypeStruct((t, D_BRANCH), BF16),
        scratch_shapes=[pltpu.VMEM((2, tq, LANES), F32)] * 2,
        compiler_params=_params("parallel", "parallel", "arbitrary"),
        name="fox_attention",
    )(qa, ka, va, proj)


def _merge_body(x_ref, ya_ref, yb_ref, yc_ref, ga_ref, gb_ref, gc_ref, wb_ref, wo_ref, o_ref):
    mixed = None
    for i, (y_ref, g_ref) in enumerate(((ya_ref, ga_ref), (yb_ref, gb_ref), (yc_ref, gc_ref))):
        d = jnp.dot(y_ref[...], wb_ref[i], preferred_element_type=F32)
        term = _sigmoid(g_ref[...].astype(F32)) * d
        mixed = term if mixed is None else mixed + term
    o_ref[...] = x_ref[...] + jnp.dot(mixed.astype(BF16), wo_ref[...], preferred_element_type=F32)


def _merge(x2, ya, yb, yc, proj, w_branch, w_out, tm):
    t = x2.shape[0]
    rows = lambda c: pl.BlockSpec((tm, D_MODEL), lambda m: (m, c))
    return pl.pallas_call(
        _merge_body,
        grid=(t // tm,),
        in_specs=[
            rows(0), rows(0), rows(0), rows(0),
            rows(COL_MERGE), rows(COL_MERGE + 1), rows(COL_MERGE + 2),
            pl.BlockSpec((N_BRANCH, D_BRANCH, D_MODEL), lambda m: (0, 0, 0)),
            pl.BlockSpec((D_MODEL, D_MODEL), lambda m: (0, 0)),
        ],
        out_specs=rows(0),
        out_shape=jax.ShapeDtypeStruct((t, D_MODEL), F32),
        compiler_params=_params("parallel"),
        name="merge_out",
    )(x2, ya, yb, yc, proj, proj, proj, w_branch, w_out)


def _tiles(seq):
    return dict(
        tm_in=min(2048, seq), tn_in=2048,
        ts_conv=min(256, seq),
        tl_hgrn=min(512, seq), nh_hgrn=4,
        ts_fox=min(512, seq),
        tq=min(1024, seq), tk=min(512, seq),
        tm_merge=min(512, seq),
    )


def kernel(x, norm_g, w_in, conv_w, conv_b, conv_ln_g, conv_ln_b, hgrn_lb_logits, hgrn_norm_g, fox_f_bias,
           fox_qn_g, fox_kn_g, w_branch, w_out):
    batch, seq, _ = x.shape
    depth = w_in.shape[0]
    tl = _tiles(seq)
    t = batch * seq

    p = jax.nn.softmax(hgrn_lb_logits.astype(F32), axis=0)
    cum = jnp.cumsum(p, axis=0)
    lower_bounds = cum - cum[0:1]

    expo_np, sel_np = _hgrn_constants()
    expo = jnp.asarray(expo_np, BF16)
    sel = jnp.asarray(sel_np)
    pair_np, pq_np, pk_np, feat_np, onev_np = _fox_constants()
    ts = tl["ts_fox"]
    tri = jnp.asarray(np.tril(np.ones((ts, ts), np.float32)), BF16)
    fox_consts = (jnp.asarray(pair_np, BF16), jnp.asarray(pq_np, BF16), jnp.asarray(pk_np, BF16),
                  jnp.asarray(feat_np, F32), jnp.asarray(onev_np, F32), tri)

    x2 = x.reshape(t, D_MODEL)
    for l in range(depth):
        w = w_in[l]
        w_main = jnp.concatenate([w[:, :C_F_OFFSET], w[:, C_F_OFFSET + FOX_HEADS:]], axis=1).astype(BF16)
        wf = w[:, C_F_OFFSET:C_F_OFFSET + FOX_HEADS]
        w_f = jnp.concatenate([wf] * N_SPLIT + [jnp.zeros((D_MODEL, LANES - N_SPLIT * FOX_HEADS), F32)],
                              axis=1).astype(BF16)
        fb = jnp.concatenate([fox_f_bias[l]] * N_SPLIT + [jnp.zeros((LANES - N_SPLIT * FOX_HEADS,), F32)])[None, :]
        qg = jnp.tile(fox_qn_g[l], 2)[None, :]
        kg = jnp.tile(fox_kn_g[l], 2)[None, :]
        cw = jnp.broadcast_to(conv_w[l][:, None, :], (CONV_K, SUBLANES, D_BRANCH))

        proj, f_logit = _inproj(x2, norm_g[l][None, :], w_main, w_f, tl["tm_in"], tl["tn_in"])
        ya = _conv_branch(proj, cw, conv_b[l][None, :], conv_ln_g[l][None, :], conv_ln_b[l][None, :],
                          batch, seq, tl["ts_conv"])
        yb = _hgrn_branch(proj, lower_bounds[l][None, :], hgrn_norm_g[l][None, :], expo, sel, batch, seq,
                          tl["tl_hgrn"], tl["nh_hgrn"])
        qa, ka, va = _fox_prologue(proj, f_logit, fb, qg, kg, fox_consts, batch, seq, ts)
        yc = _fox_attention(qa, ka, va, proj, batch, seq, tl["tq"], tl["tk"])
        x2 = _merge(x2, ya, yb, yc, proj, w_branch[l].astype(BF16), w_out[l].astype(BF16), tl["tm_merge"])
    return x2.reshape(batch, seq, D_MODEL)
```

```python
import functools
import math

import jax
import jax.numpy as jnp
import numpy as np
from jax import lax
from jax.experimental import pallas as pl
from jax.experimental.pallas import tpu as pltpu

F32 = jnp.float32
BF16 = jnp.bfloat16

D_MODEL = 1024
D_BRANCH = 1024
N_BRANCH = 3
CONV_K = 31
HGRN_HEAD = 128
HGRN_CHUNK = 64
FOX_HEAD = 64
FOX_HEADS = D_BRANCH // FOX_HEAD
EPS = 1e-6
LOG2E = math.log2(math.e)

LANES = 128
SUBLANES = 8
VMEM_LIMIT = 56 * 1024 * 1024

COL_A_VAL, COL_A_GLU, COL_A_GATE = 0, 1, 2
COL_B_Q, COL_B_F, COL_B_I, COL_B_GATE = 3, 4, 5, 6
COL_C_Q, COL_C_K, COL_C_V, COL_C_GATE = 7, 8, 9, 10
COL_MERGE = 11
N_MAIN = 14 * D_BRANCH
C_F_OFFSET = 11 * D_BRANCH

INPROJ_PIECE = 512
CONV_HALO = 32
N_SPLIT = 3
AUG_ONE_LANE = 3 * FOX_HEADS


def _sigmoid(x):
    return 0.5 * jnp.tanh(0.5 * x) + 0.5


def _silu(x):
    h = 0.5 * x
    return h * jnp.tanh(h) + h


def _log2(x):
    return LOG2E * jnp.log(x)


def _log2_sigmoid(x):
    return LOG2E * (jnp.minimum(x, 0.0) - jnp.log(1.0 + jnp.exp2(-LOG2E * jnp.abs(x))))


def _split_bf16(x, n):
    pieces = []
    r = x
    for _ in range(n):
        p = r.astype(BF16)
        pieces.append(p)
        r = r - p.astype(F32)
    return pieces


def _params(*sem):
    return pltpu.CompilerParams(dimension_semantics=sem, vmem_limit_bytes=VMEM_LIMIT)


def _inproj_body(x_ref, g_ref, w_ref, wf_ref, proj_ref, f_ref, h_ref):
    @pl.when(pl.program_id(1) == 0)
    def _():
        x = x_ref[...]
        ms = jnp.mean(x * x, axis=-1, keepdims=True)
        h = (x * lax.rsqrt(ms + EPS) * g_ref[...]).astype(BF16)
        h_ref[...] = h
        f_ref[...] = jnp.dot(h, wf_ref[...], preferred_element_type=F32)

    for c0 in range(0, proj_ref.shape[1], INPROJ_PIECE):
        cs = slice(c0, c0 + INPROJ_PIECE)
        proj_ref[:, cs] = jnp.dot(h_ref[...], w_ref[:, cs], preferred_element_type=F32).astype(BF16)


def _inproj(x2, g, w_main, w_f, tm, tn):
    t = x2.shape[0]
    return pl.pallas_call(
        _inproj_body,
        grid=(t // tm, N_MAIN // tn),
        in_specs=[
            pl.BlockSpec((tm, D_MODEL), lambda m, n: (m, 0)),
            pl.BlockSpec((1, D_MODEL), lambda m, n: (0, 0)),
            pl.BlockSpec((D_MODEL, tn), lambda m, n: (0, n)),
            pl.BlockSpec((D_MODEL, LANES), lambda m, n: (0, 0)),
        ],
        out_specs=[
            pl.BlockSpec((tm, tn), lambda m, n: (m, n)),
            pl.BlockSpec((tm, LANES), lambda m, n: (m, 0)),
        ],
        out_shape=[
            jax.ShapeDtypeStruct((t, N_MAIN), BF16),
            jax.ShapeDtypeStruct((t, LANES), F32),
        ],
        scratch_shapes=[pltpu.VMEM((tm, D_MODEL), BF16)],
        compiler_params=_params("parallel", "arbitrary"),
        name="inproj",
    )(x2, g, w_main, w_f)


def _conv_body(val_ref, glu_ref, gate_ref, w_ref, cb_ref, lg_ref, lb_ref, o_ref, u_ref, y_ref, *, ts, rc, nr):
    @pl.when(pl.program_id(1) == 0)
    def _():
        u_ref[0, 0:CONV_HALO, :] = jnp.zeros((CONV_HALO, D_BRANCH), F32)

    u_ref[0, CONV_HALO:CONV_HALO + ts, :] = val_ref[...].astype(F32) * _sigmoid(glu_ref[...].astype(F32))
    blk = 5 * SUBLANES
    for b in range(1, SUBLANES):
        for r in range(SUBLANES, CONV_HALO + ts, blk):
            n = min(blk, CONV_HALO + ts - r)
            u_ref[b, r:r + n, :] = u_ref[0, r - b:r - b + n, :]

    groups = rc // SUBLANES
    nlb = D_BRANCH // LANES
    back_groups = (CONV_K - 1) // SUBLANES

    for lb in range(nlb):
        ls = slice(lb * LANES, (lb + 1) * LANES)
        taps = [w_ref[j, :, ls][None] for j in range(CONV_K)]
        bias = jnp.broadcast_to(cb_ref[:, ls][None], (groups, SUBLANES, LANES))

        def rows(c, carry, ls=ls, taps=taps, bias=bias):
            r0 = pl.multiple_of(c * rc, rc)
            accs = [bias, None]
            for b in range(SUBLANES):
                win = u_ref[b, pl.ds(r0 + CONV_HALO - back_groups * SUBLANES, rc + back_groups * SUBLANES), ls]
                for a in range(back_groups + 1):
                    j = CONV_K - 1 - (SUBLANES * a + b)
                    if j < 0:
                        continue
                    lo = (back_groups - a) * SUBLANES
                    term = taps[j] * win[lo:lo + rc].reshape(groups, SUBLANES, LANES)
                    accs[b % 2] = term if accs[b % 2] is None else accs[b % 2] + term
            y_ref[pl.ds(r0, rc), ls] = (accs[0] + accs[1]).reshape(rc, LANES)
            return carry

        lax.fori_loop(0, ts // rc, rows, 0)

    def norm(c, carry):
        r0 = pl.multiple_of(c * nr, nr)
        y = y_ref[pl.ds(r0, nr), :]
        mu = jnp.mean(y, axis=-1, keepdims=True)
        d = y - mu
        var = jnp.mean(d * d, axis=-1, keepdims=True)
        y = d * lax.rsqrt(var + EPS) * lg_ref[...] + lb_ref[...]
        y = _silu(y) * _silu(gate_ref[pl.ds(r0, nr), :].astype(F32))
        o_ref[pl.ds(r0, nr), :] = y.astype(BF16)
        return carry

    lax.fori_loop(0, ts // nr, norm, 0)
    u_ref[0, 0:CONV_HALO, :] = u_ref[0, ts:ts + CONV_HALO, :]


def _conv_branch(proj, conv_w, conv_b, ln_g, ln_b, batch, seq, ts, rc=64, nr=64):
    t = batch * seq
    ns = seq // ts
    row = lambda b, s: b * ns + s
    vec = pl.BlockSpec((1, D_BRANCH), lambda b, s: (0, 0))
    return pl.pallas_call(
        functools.partial(_conv_body, ts=ts, rc=rc, nr=nr),
        grid=(batch, ns),
        in_specs=[
            pl.BlockSpec((ts, D_BRANCH), lambda b, s: (row(b, s), COL_A_VAL)),
            pl.BlockSpec((ts, D_BRANCH), lambda b, s: (row(b, s), COL_A_GLU)),
            pl.BlockSpec((ts, D_BRANCH), lambda b, s: (row(b, s), COL_A_GATE)),
            pl.BlockSpec((CONV_K, SUBLANES, D_BRANCH), lambda b, s: (0, 0, 0)),
            vec, vec, vec,
        ],
        out_specs=pl.BlockSpec((ts, D_BRANCH), lambda b, s: (row(b, s), 0)),
        out_shape=jax.ShapeDtypeStruct((t, D_BRANCH), BF16),
        scratch_shapes=[pltpu.VMEM((SUBLANES, ts + CONV_HALO, D_BRANCH), F32),
                        pltpu.VMEM((ts, D_BRANCH), F32)],
        compiler_params=_params("parallel", "arbitrary"),
        name="conv_branch",
    )(proj, proj, proj, conv_w, conv_b, ln_g, ln_b)


HGRN_MATMUL_LEVELS = (2, 4)
HGRN_BCAST_LEVELS = (8, 16, 32)
HGRN_PRODUCTS = 7
HGRN_UNROLL = 8
HGRN_DIRECT_SPAN = 100.0


def _hgrn_constants():
    c = HGRN_CHUNK
    t = np.arange(c)[:, None]
    j = np.arange(c)[None, :]
    mats = [(j <= t)]
    masks = [np.eye(c, dtype=bool)]
    for m in (1,) + HGRN_MATMUL_LEVELS + HGRN_BCAST_LEVELS:
        mid = (t // (2 * m)) * 2 * m + m
        upper = t >= mid
        if m in HGRN_MATMUL_LEVELS:
            e_upper = upper & (j >= mid) & (j <= t)
            e_lower = (~upper) & (j > t) & (j < mid)
            mats.append(e_upper | e_lower)
        masks.append((t // (2 * m) == j // (2 * m)) & (t % (2 * m) >= m) & (j % (2 * m) < m))
    expo = np.concatenate(mats, axis=0).astype(np.float32)
    assert len(masks) == HGRN_PRODUCTS
    sel = np.full((c, c), -1, np.int32)
    for i, mk in enumerate(masks):
        sel[mk] = i
    return expo, sel


def _hgrn_body(q_ref, f_ref, i_ref, gate_ref, lb_ref, ng_ref, expo_ref, sel_ref, o_ref, st_ref, c_ref, *,
               nchunk, nh):
    c = HGRN_CHUNK
    dh = HGRN_HEAD
    gw = nh * dh

    @pl.when(pl.program_id(2) == 0)
    def _():
        st_ref[...] = jnp.zeros((nh, dh, dh), F32)

    lbf = lb_ref[...]
    half_span = 0.5 * (1.0 - lbf)
    log_lb = _log2(lbf)
    log1m_lb = _log2(1.0 - lbf)
    sel = sel_ref[...]
    nt = (((1,), (1,)), ((), ()))
    tn = (((0,), (0,)), ((), ()))

    def prepare(ci, slot):
        r0 = pl.multiple_of(ci * c, c)
        z = f_ref[pl.ds(r0, c), :].astype(F32)
        qf = _silu(q_ref[pl.ds(r0, c), :].astype(F32))
        bt = half_span * jnp.tanh(0.5 * z)
        f = (lbf + half_span) + bt
        k = half_span - bt
        b = log1m_lb + _log2_sigmoid(z)
        log2_f = jnp.maximum(log_lb, b) + _log2(1.0 + jnp.exp2(-jnp.abs(log_lb - b)))
        pieces = jnp.concatenate(_split_bf16(log2_f, 2), axis=1)
        g = jnp.dot(expo_ref[0:c, :], pieces, preferred_element_type=F32)
        c2 = g[:, :gw] + g[:, gw:]
        c_ref[slot] = c2
        c_last = c_ref[slot, pl.ds(c - 1, 1), :]
        w_read = jnp.exp2(c2)
        qb = qf.astype(BF16)
        kb = k.astype(BF16)
        return dict(
            r0=r0, slot=slot, c2=c2, c_last=c_last, qb=qb, kb=kb, f=f, pieces=pieces,
            decay=w_read[c - 1:c, :],
            q_state=qb * w_read.astype(BF16),
            k_state=kb * jnp.exp2(c_last - c2).astype(BF16),
            gate=_silu(gate_ref[pl.ds(r0, c), :].astype(F32)))

    def finish(pre, scores):
        r0 = pre["r0"]
        for h in range(nh):
            hs = slice(h * dh, (h + 1) * dh)
            v = i_ref[pl.ds(r0, c), hs]
            st = st_ref[h]
            lhs = jnp.concatenate([pre["q_state"][:, hs], scores(hs).astype(BF16)], axis=1)
            rhs = jnp.concatenate([st.astype(BF16), v], axis=0)
            o = jnp.dot(lhs, rhs, preferred_element_type=F32)
            dcol = jnp.broadcast_to(pre["decay"][:, hs], (dh, dh)).T
            st_ref[h] = st * dcol + lax.dot_general(pre["k_state"][:, hs], v, tn, preferred_element_type=F32)
            o = o * lax.rsqrt(jnp.mean(o * o, axis=-1, keepdims=True) + EPS) * ng_ref[:, hs]
            o_ref[pl.ds(r0, c), hs] = (o * pre["gate"][:, hs]).astype(BF16)

    def chunk_direct(pre):
        y = pre["kb"] * jnp.exp2(-pre["c2"]).astype(BF16)

        def scores(hs):
            pp = lax.dot_general(pre["q_state"][:, hs], y[:, hs], nt, preferred_element_type=F32)
            return jnp.where(sel >= 0, pp, 0.0)

        finish(pre, scores)

    def chunk_dyadic(pre):
        c2, qb, kb, slot = pre["c2"], pre["qb"], pre["kb"], pre["slot"]
        g = jnp.dot(expo_ref[c:, :], pre["pieces"], preferred_element_type=F32)
        g = g[:, :gw] + g[:, gw:]
        wl = [jnp.exp2(g[i * c:(i + 1) * c]) for i in range(len(HGRN_MATMUL_LEVELS))]
        for m in HGRN_BCAST_LEVELS:
            segs = [-jnp.abs(c2[s0:s0 + 2 * m] - c_ref[slot, pl.ds(s0 + m - 1, 1), :]) for s0 in range(0, c, 2 * m)]
            wl.append(jnp.exp2(jnp.concatenate(segs, axis=0) if len(segs) > 1 else segs[0]))
        wlb = [x.astype(BF16) for x in wl]
        xs = [qb, qb * pre["f"].astype(BF16)] + [qb * x for x in wlb]
        ys = [kb, kb] + [kb * x for x in wlb]

        def scores(hs):
            a = None
            for p in range(HGRN_PRODUCTS):
                pp = lax.dot_general(xs[p][:, hs], ys[p][:, hs], nt, preferred_element_type=F32)
                a = jnp.where(sel == p, pp, 0.0 if a is None else a)
            return a

        finish(pre, scores)

    def group(i, carry):
        pres = [prepare(i * HGRN_UNROLL + u, u) for u in range(HGRN_UNROLL)]
        span = pres[0]["c_last"]
        for pre in pres[1:]:
            span = jnp.minimum(span, pre["c_last"])
        small = jnp.min(span) >= -HGRN_DIRECT_SPAN

        @pl.when(small)
        def _():
            for pre in pres:
                chunk_direct(pre)

        @pl.when(jnp.logical_not(small))
        def _():
            for pre in pres:
                chunk_dyadic(pre)

        return carry

    lax.fori_loop(0, nchunk // HGRN_UNROLL, group, 0)


def _hgrn_branch(proj, lb, norm_g, expo, sel, batch, seq, tl, nh):
    t = batch * seq
    nl = seq // tl
    gw = nh * HGRN_HEAD
    groups = D_BRANCH // gw
    col = lambda base: (lambda b, h, s: (b * nl + s, base * groups + h))
    vec = pl.BlockSpec((1, gw), lambda b, h, s: (0, h))
    return pl.pallas_call(
        functools.partial(_hgrn_body, nchunk=tl // HGRN_CHUNK, nh=nh),
        grid=(batch, groups, nl),
        in_specs=[
            pl.BlockSpec((tl, gw), col(COL_B_Q)),
            pl.BlockSpec((tl, gw), col(COL_B_F)),
            pl.BlockSpec((tl, gw), col(COL_B_I)),
            pl.BlockSpec((tl, gw), col(COL_B_GATE)),
            vec, vec,
            pl.BlockSpec(((1 + len(HGRN_MATMUL_LEVELS)) * HGRN_CHUNK, HGRN_CHUNK), lambda b, h, s: (0, 0)),
            pl.BlockSpec((HGRN_CHUNK, HGRN_CHUNK), lambda b, h, s: (0, 0)),
        ],
        out_specs=pl.BlockSpec((tl, gw), lambda b, h, s: (b * nl + s, h)),
        out_shape=jax.ShapeDtypeStruct((t, D_BRANCH), BF16),
        scratch_shapes=[pltpu.VMEM((nh, HGRN_HEAD, HGRN_HEAD), F32),
                        pltpu.VMEM((HGRN_UNROLL, HGRN_CHUNK, gw), F32)],
        compiler_params=_params("parallel", "parallel", "arbitrary"),
        name="hgrn_branch",
    )(proj, proj, proj, proj, lb, norm_g, expo, sel)


def _fox_constants():
    pair = np.zeros((LANES, LANES), np.float32)
    pair[:FOX_HEAD, :FOX_HEAD] = 1.0 / FOX_HEAD
    pair[FOX_HEAD:, FOX_HEAD:] = 1.0 / FOX_HEAD
    wide = FOX_HEADS * LANES
    place_q = np.zeros((LANES, wide), np.float32)
    place_k = np.zeros((LANES, wide), np.float32)
    feat = np.zeros((1, wide), np.float32)
    one_v = np.zeros((1, wide), np.float32)
    for h in range(FOX_HEADS):
        lo = h * LANES + (0 if h % 2 == 0 else FOX_HEAD)
        feat[0, lo:lo + FOX_HEAD] = 1.0
        aug = h * LANES + (FOX_HEAD if h % 2 == 0 else 0)
        one_v[0, aug] = 1.0
        for p in range(N_SPLIT):
            place_q[p * FOX_HEADS + h, aug + p] = 1.0
            place_q[AUG_ONE_LANE, aug + N_SPLIT + p] = 1.0
            place_k[AUG_ONE_LANE, aug + p] = 1.0
            place_k[p * FOX_HEADS + h, aug + N_SPLIT + p] = -1.0
    return pair, place_q, place_k, feat, one_v


def _fox_pre_body(q_ref, k_ref, v_ref, fl_ref, fb_ref, qg_ref, kg_ref, pair_ref, pq_ref, pk_ref, feat_ref,
                  onev_ref, tri_ref, qo_ref, ko_ref, vo_ref, carry_ref, *, ts):
    @pl.when(pl.program_id(1) == 0)
    def _():
        carry_ref[...] = jnp.zeros((1, LANES), F32)

    ls = _log2_sigmoid(fl_ref[...] + fb_ref[...])
    tri = tri_ref[...]
    fcum = jnp.zeros((ts, LANES), F32)
    for p in _split_bf16(ls, N_SPLIT):
        fcum = fcum + jnp.dot(tri, p, preferred_element_type=F32)
    fcum = fcum + carry_ref[...]
    carry_ref[...] = fcum[ts - 1:ts, :]

    lane = lax.broadcasted_iota(jnp.int32, (ts, LANES), 1)
    p0, p1, p2 = _split_bf16(fcum, N_SPLIT)
    pieces = jnp.where(lane < FOX_HEADS, p0.astype(F32),
                       jnp.where(lane < 2 * FOX_HEADS, p1.astype(F32),
                                 jnp.where(lane < 3 * FOX_HEADS, p2.astype(F32),
                                           jnp.where(lane == AUG_ONE_LANE, 1.0, 0.0)))).astype(BF16)

    scale = LOG2E / math.sqrt(FOX_HEAD)
    for src_ref, g_ref, place_ref, dst_ref, mul in ((q_ref, qg_ref, pq_ref, qo_ref, scale), (k_ref, kg_ref, pk_ref, ko_ref, 1.0)):
        for pr in range(FOX_HEADS // 2):
            sl = slice(pr * LANES, (pr + 1) * LANES)
            x = src_ref[:, sl].astype(F32)
            ms = jnp.dot((x * x).astype(BF16), pair_ref[...], preferred_element_type=F32)
            xn = x * lax.rsqrt(ms + EPS) * (g_ref[...] * mul)
            for h in (2 * pr, 2 * pr + 1):
                hs = slice(h * LANES, (h + 1) * LANES)
                aug = jnp.dot(pieces, place_ref[:, hs], preferred_element_type=F32)
                dst_ref[:, hs] = (xn * feat_ref[:, hs] + aug).astype(BF16)
    for pr in range(FOX_HEADS // 2):
        v = v_ref[:, pr * LANES:(pr + 1) * LANES].astype(F32)
        for h in (2 * pr, 2 * pr + 1):
            hs = slice(h * LANES, (h + 1) * LANES)
            vo_ref[:, hs] = (v * feat_ref[:, hs] + onev_ref[:, hs]).astype(BF16)


def _fox_prologue(proj, f_logit, f_bias, qn_g, kn_g, consts, batch, seq, ts):
    t = batch * seq
    ns = seq // ts
    pair, place_q, place_k, feat, one_v, tri = consts
    wide = FOX_HEADS * LANES
    row = lambda b, s: b * ns + s
    const = lambda shape: pl.BlockSpec(shape, lambda b, s: (0, 0))
    wide_out = pl.BlockSpec((ts, wide), lambda b, s: (row(b, s), 0))
    return pl.pallas_call(
        functools.partial(_fox_pre_body, ts=ts),
        grid=(batch, ns),
        in_specs=[
            pl.BlockSpec((ts, D_BRANCH), lambda b, s: (row(b, s), COL_C_Q)),
            pl.BlockSpec((ts, D_BRANCH), lambda b, s: (row(b, s), COL_C_K)),
            pl.BlockSpec((ts, D_BRANCH), lambda b, s: (row(b, s), COL_C_V)),
            pl.BlockSpec((ts, LANES), lambda b, s: (row(b, s), 0)),
            const((1, LANES)), const((1, LANES)), const((1, LANES)),
            const((LANES, LANES)), const((LANES, wide)), const((LANES, wide)), const((1, wide)), const((1, wide)),
            const((ts, ts)),
        ],
        out_specs=[wide_out, wide_out, wide_out],
        out_shape=[jax.ShapeDtypeStruct((t, wide), BF16)] * 3,
        scratch_shapes=[pltpu.VMEM((1, LANES), F32)],
        compiler_params=_params("parallel", "arbitrary"),
        name="fox_prologue",
    )(proj, proj, proj, f_logit, f_bias, qn_g, kn_g, pair, place_q, place_k, feat, one_v, tri)


def _fox_body(q_ref, k_ref, v_ref, gate_ref, o_ref, m_ref, acc_ref, *, tq, tk):
    qi = pl.program_id(2)
    nt = (((1,), (1,)), ((), ()))

    for hh in range(2):
        m_ref[hh] = jnp.full((tq, LANES), -jnp.inf, F32)
        acc_ref[hh] = jnp.zeros((tq, LANES), F32)

    def step(j, r0, nr, nkb=1, visible=None):
        k0 = pl.multiple_of(j * tk, tk)
        kw = nkb * tk
        rs = slice(r0, r0 + nr)
        for hh in range(2):
            hs = slice(hh * LANES, (hh + 1) * LANES)
            s = lax.dot_general(q_ref[rs, hs], k_ref[pl.ds(k0, kw), hs], nt, preferred_element_type=F32)
            if visible is not None:
                rows = lax.broadcasted_iota(jnp.int32, (tk, kw), 0)
                cols = lax.broadcasted_iota(jnp.int32, (tk, kw), 1)
                edge = jnp.where(cols <= rows + visible, s[:tk], -jnp.inf)
                s = edge if nr == tk else jnp.concatenate([edge, s[tk:]], axis=0)
            blocks = [s[:, cb * LANES:(cb + 1) * LANES] for cb in range(kw // LANES)]
            mx = blocks[0]
            for blk in blocks[1:]:
                mx = jnp.maximum(mx, blk)
            m_prev = m_ref[hh, rs]
            m_new = jnp.maximum(m_prev, jnp.max(mx, axis=-1, keepdims=True))
            alpha = jnp.exp2(m_prev - m_new)
            p = jnp.concatenate([jnp.exp2(blk - m_new).astype(BF16) for blk in blocks], axis=1)
            acc_ref[hh, rs] = alpha * acc_ref[hh, rs] + jnp.dot(p, v_ref[pl.ds(k0, kw), hs],
                                                                preferred_element_type=F32)
            m_ref[hh, rs] = m_new

    nsub = tq // tk

    def body(jj, carry):
        step(2 * jj, 0, tq, nkb=2)
        return carry

    nfull = qi * nsub
    lax.fori_loop(0, nfull // 2, body, 0)
    if nsub % 2 == 1:
        @pl.when(nfull % 2 == 1)
        def _():
            step(nfull - 1, 0, tq)

    for a in range(nsub):
        step(nfull + a, a * tk, tq - a * tk, visible=0)

    lane = lax.broadcasted_iota(jnp.int32, (tq, LANES), 1)
    acc0 = acc_ref[0]
    acc1 = acc_ref[1]
    out = jnp.where(lane < FOX_HEAD, acc0 / acc0[:, FOX_HEAD:FOX_HEAD + 1], acc1 / acc1[:, 0:1])
    o_ref[...] = (out * _silu(gate_ref[...].astype(F32))).astype(BF16)


def _fox_attention(qa, ka, va, proj, batch, seq, tq, tk):
    t = batch * seq
    nq = seq // tq
    pairs = FOX_HEADS // 2
    kv = pl.BlockSpec((seq, 2 * LANES), lambda b, p, i: (b, p))
    return pl.pallas_call(
        functools.partial(_fox_body, tq=tq, tk=tk),
        grid=(batch, pairs, nq),
        in_specs=[
            pl.BlockSpec((tq, 2 * LANES), lambda b, p, i: (b * nq + i, p)),
            kv, kv,
            pl.BlockSpec((tq, LANES), lambda b, p, i: (b * nq + i, COL_C_GATE * pairs + p)),
        ],
        out_specs=pl.BlockSpec((tq, LANES), lambda b, p, i: (b * nq + i, p)),
        out_shape=jax.ShapeDtypeStruct((t, D_BRANCH), BF16),
        scratch_shapes=[pltpu.VMEM((2, tq, LANES), F32)] * 2,
        compiler_params=_params("parallel", "parallel", "arbitrary"),
        name="fox_attention",
    )(qa, ka, va, proj)


def _merge_body(x_ref, ya_ref, yb_ref, yc_ref, ga_ref, gb_ref, gc_ref, wb_ref, wo_ref, o_ref):
    mixed = None
    for i, (y_ref, g_ref) in enumerate(((ya_ref, ga_ref), (yb_ref, gb_ref), (yc_ref, gc_ref))):
        d = jnp.dot(y_ref[...], wb_ref[i], preferred_element_type=F32)
        term = _sigmoid(g_ref[...].astype(F32)) * d
        mixed = term if mixed is None else mixed + term
    o_ref[...] = x_ref[...] + jnp.dot(mixed.astype(BF16), wo_ref[...], preferred_element_type=F32)


def _merge(x2, ya, yb, yc, proj, w_branch, w_out, tm):
    t = x2.shape[0]
    rows = lambda c: pl.BlockSpec((tm, D_MODEL), lambda m: (m, c))
    return pl.pallas_call(
        _merge_body,
        grid=(t // tm,),
        in_specs=[
            rows(0), rows(0), rows(0), rows(0),
            rows(COL_MERGE), rows(COL_MERGE + 1), rows(COL_MERGE + 2),
            pl.BlockSpec((N_BRANCH, D_BRANCH, D_MODEL), lambda m: (0, 0, 0)),
            pl.BlockSpec((D_MODEL, D_MODEL), lambda m: (0, 0)),
        ],
        out_specs=rows(0),
        out_shape=jax.ShapeDtypeStruct((t, D_MODEL), F32),
        compiler_params=_params("parallel"),
        name="merge_out",
    )(x2, ya, yb, yc, proj, proj, proj, w_branch, w_out)


def _tiles(seq):
    return dict(
        tm_in=min(2048, seq), tn_in=2048,
        ts_conv=min(256, seq),
        tl_hgrn=min(512, seq), nh_hgrn=4,
        ts_fox=min(512, seq),
        tq=min(1024, seq), tk=min(512, seq),
        tm_merge=min(512, seq),
    )


def kernel(x, norm_g, w_in, conv_w, conv_b, conv_ln_g, conv_ln_b, hgrn_lb_logits, hgrn_norm_g, fox_f_bias,
           fox_qn_g, fox_kn_g, w_branch, w_out):
    batch, seq, _ = x.shape
    depth = w_in.shape[0]
    tl = _tiles(seq)
    t = batch * seq

    p = jax.nn.softmax(hgrn_lb_logits.astype(F32), axis=0)
    cum = jnp.cumsum(p, axis=0)
    lower_bounds = cum - cum[0:1]

    expo_np, sel_np = _hgrn_constants()
    expo = jnp.asarray(expo_np, BF16)
    sel = jnp.asarray(sel_np)
    pair_np, pq_np, pk_np, feat_np, onev_np = _fox_constants()
    ts = tl["ts_fox"]
    tri = jnp.asarray(np.tril(np.ones((ts, ts), np.float32)), BF16)
    fox_consts = (jnp.asarray(pair_np, BF16), jnp.asarray(pq_np, BF16), jnp.asarray(pk_np, BF16),
                  jnp.asarray(feat_np, F32), jnp.asarray(onev_np, F32), tri)

    x2 = x.reshape(t, D_MODEL)
    for l in range(depth):
        w = w_in[l]
        w_main = jnp.concatenate([w[:, :C_F_OFFSET], w[:, C_F_OFFSET + FOX_HEADS:]], axis=1).astype(BF16)
        wf = w[:, C_F_OFFSET:C_F_OFFSET + FOX_HEADS]
        w_f = jnp.concatenate([wf] * N_SPLIT + [jnp.zeros((D_MODEL, LANES - N_SPLIT * FOX_HEADS), F32)],
                              axis=1).astype(BF16)
        fb = jnp.concatenate([fox_f_bias[l]] * N_SPLIT + [jnp.zeros((LANES - N_SPLIT * FOX_HEADS,), F32)])[None, :]
        qg = jnp.tile(fox_qn_g[l], 2)[None, :]
        kg = jnp.tile(fox_kn_g[l], 2)[None, :]
        cw = jnp.broadcast_to(conv_w[l][:, None, :], (CONV_K, SUBLANES, D_BRANCH))

        proj, f_logit = _inproj(x2, norm_g[l][None, :], w_main, w_f, tl["tm_in"], tl["tn_in"])
        ya = _conv_branch(proj, cw, conv_b[l][None, :], conv_ln_g[l][None, :], conv_ln_b[l][None, :],
                          batch, seq, tl["ts_conv"])
        yb = _hgrn_branch(proj, lower_bounds[l][None, :], hgrn_norm_g[l][None, :], expo, sel, batch, seq,
                          tl["tl_hgrn"], tl["nh_hgrn"])
        qa, ka, va = _fox_prologue(proj, f_logit, fb, qg, kg, fox_consts, batch, seq, ts)
        yc = _fox_attention(qa, ka, va, proj, batch, seq, tl["tq"], tl["tk"])
        x2 = _merge(x2, ya, yb, yc, proj, w_branch[l].astype(BF16), w_out[l].astype(BF16), tl["tm_merge"])
    return x2.reshape(batch, seq, D_MODEL)
```

```python
import functools
import math

import jax
import jax.numpy as jnp
import numpy as np
from jax import lax
from jax.experimental import pallas as pl
from jax.experimental.pallas import tpu as pltpu

F32 = jnp.float32
BF16 = jnp.bfloat16

D_MODEL = 1024
D_BRANCH = 1024
N_BRANCH = 3
CONV_K = 31
HGRN_HEAD = 128
HGRN_CHUNK = 64
FOX_HEAD = 64
FOX_HEADS = D_BRANCH // FOX_HEAD
EPS = 1e-6
LOG2E = math.log2(math.e)

LANES = 128
SUBLANES = 8
VMEM_LIMIT = 56 * 1024 * 1024

COL_A_VAL, COL_A_GLU, COL_A_GATE = 0, 1, 2
COL_B_Q, COL_B_F, COL_B_I, COL_B_GATE = 3, 4, 5, 6
COL_C_Q, COL_C_K, COL_C_V, COL_C_GATE = 7, 8, 9, 10
COL_MERGE = 11
N_MAIN = 14 * D_BRANCH
C_F_OFFSET = 11 * D_BRANCH

INPROJ_PIECE = 512
CONV_HALO = 32
N_SPLIT = 3
AUG_ONE_LANE = 3 * FOX_HEADS


def _sigmoid(x):
    return 0.5 * jnp.tanh(0.5 * x) + 0.5


def _silu(x):
    h = 0.5 * x
    return h * jnp.tanh(h) + h


def _log2(x):
    return LOG2E * jnp.log(x)


def _log2_sigmoid(x):
    return LOG2E * (jnp.minimum(x, 0.0) - jnp.log(1.0 + jnp.exp2(-LOG2E * jnp.abs(x))))


def _split_bf16(x, n):
    pieces = []
    r = x
    for _ in range(n):
        p = r.astype(BF16)
        pieces.append(p)
        r = r - p.astype(F32)
    return pieces


def _params(*sem):
    return pltpu.CompilerParams(dimension_semantics=sem, vmem_limit_bytes=VMEM_LIMIT)


def _inproj_body(x_ref, g_ref, w_ref, wf_ref, proj_ref, f_ref, h_ref):
    @pl.when(pl.program_id(1) == 0)
    def _():
        x = x_ref[...]
        ms = jnp.mean(x * x, axis=-1, keepdims=True)
        h = (x * lax.rsqrt(ms + EPS) * g_ref[...]).astype(BF16)
        h_ref[...] = h
        f_ref[...] = jnp.dot(h, wf_ref[...], preferred_element_type=F32)

    for c0 in range(0, proj_ref.shape[1], INPROJ_PIECE):
        cs = slice(c0, c0 + INPROJ_PIECE)
        proj_ref[:, cs] = jnp.dot(h_ref[...], w_ref[:, cs], preferred_element_type=F32).astype(BF16)


def _inproj(x2, g, w_main, w_f, tm, tn):
    t = x2.shape[0]
    return pl.pallas_call(
        _inproj_body,
        grid=(t // tm, N_MAIN // tn),
        in_specs=[
            pl.BlockSpec((tm, D_MODEL), lambda m, n: (m, 0)),
            pl.BlockSpec((1, D_MODEL), lambda m, n: (0, 0)),
            pl.BlockSpec((D_MODEL, tn), lambda m, n: (0, n)),
            pl.BlockSpec((D_MODEL, LANES), lambda m, n: (0, 0)),
        ],
        out_specs=[
            pl.BlockSpec((tm, tn), lambda m, n: (m, n)),
            pl.BlockSpec((tm, LANES), lambda m, n: (m, 0)),
        ],
        out_shape=[
            jax.ShapeDtypeStruct((t, N_MAIN), BF16),
            jax.ShapeDtypeStruct((t, LANES), F32),
        ],
        scratch_shapes=[pltpu.VMEM((tm, D_MODEL), BF16)],
        compiler_params=_params("parallel", "arbitrary"),
        name="inproj",
    )(x2, g, w_main, w_f)


def _conv_body(val_ref, glu_ref, gate_ref, w_ref, cb_ref, lg_ref, lb_ref, o_ref, u_ref, y_ref, *, ts, rc, nr):
    @pl.when(pl.program_id(1) == 0)
    def _():
        u_ref[0, 0:CONV_HALO, :] = jnp.zeros((CONV_HALO, D_BRANCH), F32)

    u_ref[0, CONV_HALO:CONV_HALO + ts, :] = val_ref[...].astype(F32) * _sigmoid(glu_ref[...].astype(F32))
    blk = 5 * SUBLANES
    for b in range(1, SUBLANES):
        for r in range(SUBLANES, CONV_HALO + ts, blk):
            n = min(blk, CONV_HALO + ts - r)
            u_ref[b, r:r + n, :] = u_ref[0, r - b:r - b + n, :]

    groups = rc // SUBLANES
    nlb = D_BRANCH // LANES
    back_groups = (CONV_K - 1) // SUBLANES

    for lb in range(nlb):
        ls = slice(lb * LANES, (lb + 1) * LANES)
        taps = [w_ref[j, :, ls][None] for j in range(CONV_K)]
        bias = jnp.broadcast_to(cb_ref[:, ls][None], (groups, SUBLANES, LANES))

        def rows(c, carry, ls=ls, taps=taps, bias=bias):
            r0 = pl.multiple_of(c * rc, rc)
            accs = [bias, None]
            for b in range(SUBLANES):
                win = u_ref[b, pl.ds(r0 + CONV_HALO - back_groups * SUBLANES, rc + back_groups * SUBLANES), ls]
                for a in range(back_groups + 1):
                    j = CONV_K - 1 - (SUBLANES * a + b)
                    if j < 0:
                        continue
                    lo = (back_groups - a) * SUBLANES
                    term = taps[j] * win[lo:lo + rc].reshape(groups, SUBLANES, LANES)
                    accs[b % 2] = term if accs[b % 2] is None else accs[b % 2] + term
            y_ref[pl.ds(r0, rc), ls] = (accs[0] + accs[1]).reshape(rc, LANES)
            return carry

        lax.fori_loop(0, ts // rc, rows, 0)

    def norm(c, carry):
        r0 = pl.multiple_of(c * nr, nr)
        y = y_ref[pl.ds(r0, nr), :]
        mu = jnp.mean(y, axis=-1, keepdims=True)
        d = y - mu
        var = jnp.mean(d * d, axis=-1, keepdims=True)
        y = d * lax.rsqrt(var + EPS) * lg_ref[...] + lb_ref[...]
        y = _silu(y) * _silu(gate_ref[pl.ds(r0, nr), :].astype(F32))
        o_ref[pl.ds(r0, nr), :] = y.astype(BF16)
        return carry

    lax.fori_loop(0, ts // nr, norm, 0)
    u_ref[0, 0:CONV_HALO, :] = u_ref[0, ts:ts + CONV_HALO, :]


def _conv_branch(proj, conv_w, conv_b, ln_g, ln_b, batch, seq, ts, rc=64, nr=64):
    t = batch * seq
    ns = seq // ts
    row = lambda b, s: b * ns + s
    vec = pl.BlockSpec((1, D_BRANCH), lambda b, s: (0, 0))
    return pl.pallas_call(
        functools.partial(_conv_body, ts=ts, rc=rc, nr=nr),
        grid=(batch, ns),
        in_specs=[
            pl.BlockSpec((ts, D_BRANCH), lambda b, s: (row(b, s), COL_A_VAL)),
            pl.BlockSpec((ts, D_BRANCH), lambda b, s: (row(b, s), COL_A_GLU)),
            pl.BlockSpec((ts, D_BRANCH), lambda b, s: (row(b, s), COL_A_GATE)),
            pl.BlockSpec((CONV_K, SUBLANES, D_BRANCH), lambda b, s: (0, 0, 0)),
            vec, vec, vec,
        ],
        out_specs=pl.BlockSpec((ts, D_BRANCH), lambda b, s: (row(b, s), 0)),
        out_shape=jax.ShapeDtypeStruct((t, D_BRANCH), BF16),
        scratch_shapes=[pltpu.VMEM((SUBLANES, ts + CONV_HALO, D_BRANCH), F32),
                        pltpu.VMEM((ts, D_BRANCH), F32)],
        compiler_params=_params("parallel", "arbitrary"),
        name="conv_branch",
    )(proj, proj, proj, conv_w, conv_b, ln_g, ln_b)


HGRN_MATMUL_LEVELS = (2, 4)
HGRN_BCAST_LEVELS = (8, 16, 32)
HGRN_PRODUCTS = 7
HGRN_UNROLL = 8
HGRN_DIRECT_SPAN = 100.0


def _hgrn_constants():
    c = HGRN_CHUNK
    t = np.arange(c)[:, None]
    j = np.arange(c)[None, :]
    mats = [(j <= t)]
    masks = [np.eye(c, dtype=bool)]
    for m in (1,) + HGRN_MATMUL_LEVELS + HGRN_BCAST_LEVELS:
        mid = (t // (2 * m)) * 2 * m + m
        upper = t >= mid
        if m in HGRN_MATMUL_LEVELS:
            e_upper = upper & (j >= mid) & (j <= t)
            e_lower = (~upper) & (j > t) & (j < mid)
            mats.append(e_upper | e_lower)
        masks.append((t // (2 * m) == j // (2 * m)) & (t % (2 * m) >= m) & (j % (2 * m) < m))
    expo = np.concatenate(mats, axis=0).astype(np.float32)
    assert len(masks) == HGRN_PRODUCTS
    sel = np.full((c, c), -1, np.int32)
    for i, mk in enumerate(masks):
        sel[mk] = i
    return expo, sel


def _hgrn_body(q_ref, f_ref, i_ref, gate_ref, lb_ref, ng_ref, expo_ref, sel_ref, o_ref, st_ref, c_ref, *,
               nchunk, nh):
    c = HGRN_CHUNK
    dh = HGRN_HEAD
    gw = nh * dh

    @pl.when(pl.program_id(2) == 0)
    def _():
        st_ref[...] = jnp.zeros((nh, dh, dh), F32)

    lbf = lb_ref[...]
    half_span = 0.5 * (1.0 - lbf)
    log_lb = _log2(lbf)
    log1m_lb = _log2(1.0 - lbf)
    sel = sel_ref[...]
    nt = (((1,), (1,)), ((), ()))
    tn = (((0,), (0,)), ((), ()))

    def prepare(ci, slot):
        r0 = pl.multiple_of(ci * c, c)
        z = f_ref[pl.ds(r0, c), :].astype(F32)
        qf = _silu(q_ref[pl.ds(r0, c), :].astype(F32))
        bt = half_span * jnp.tanh(0.5 * z)
        f = (lbf + half_span) + bt
        k = half_span - bt
        b = log1m_lb + _log2_sigmoid(z)
        log2_f = jnp.maximum(log_lb, b) + _log2(1.0 + jnp.exp2(-jnp.abs(log_lb - b)))
        pieces = jnp.concatenate(_split_bf16(log2_f, 2), axis=1)
        g = jnp.dot(expo_ref[0:c, :], pieces, preferred_element_type=F32)
        c2 = g[:, :gw] + g[:, gw:]
        c_ref[slot] = c2
        c_last = c_ref[slot, pl.ds(c - 1, 1), :]
        w_read = jnp.exp2(c2)
        qb = qf.astype(BF16)
        kb = k.astype(BF16)
        return dict(
            r0=r0, slot=slot, c2=c2, c_last=c_last, qb=qb, kb=kb, f=f, pieces=pieces,
            decay=w_read[c - 1:c, :],
            q_state=qb * w_read.astype(BF16),
            k_state=kb * jnp.exp2(c_last - c2).astype(BF16),
            gate=_silu(gate_ref[pl.ds(r0, c), :].astype(F32)))

    def finish(pre, scores):
        r0 = pre["r0"]
        for h in range(nh):
            hs = slice(h * dh, (h + 1) * dh)
            v = i_ref[pl.ds(r0, c), hs]
            st = st_ref[h]
            lhs = jnp.concatenate([pre["q_state"][:, hs], scores(hs).astype(BF16)], axis=1)
            rhs = jnp.concatenate([st.astype(BF16), v], axis=0)
            o = jnp.dot(lhs, rhs, preferred_element_type=F32)
            dcol = jnp.broadcast_to(pre["decay"][:, hs], (dh, dh)).T
            st_ref[h] = st * dcol + lax.dot_general(pre["k_state"][:, hs], v, tn, preferred_element_type=F32)
            o = o * lax.rsqrt(jnp.mean(o * o, axis=-1, keepdims=True) + EPS) * ng_ref[:, hs]
            o_ref[pl.ds(r0, c), hs] = (o * pre["gate"][:, hs]).astype(BF16)

    def chunk_direct(pre):
        y = pre["kb"] * jnp.exp2(-pre["c2"]).astype(BF16)

        def scores(hs):
            pp = lax.dot_general(pre["q_state"][:, hs], y[:, hs], nt, preferred_element_type=F32)
            return jnp.where(sel >= 0, pp, 0.0)

        finish(pre, scores)

    def chunk_dyadic(pre):
        c2, qb, kb, slot = pre["c2"], pre["qb"], pre["kb"], pre["slot"]
        g = jnp.dot(expo_ref[c:, :], pre["pieces"], preferred_element_type=F32)
        g = g[:, :gw] + g[:, gw:]
        wl = [jnp.exp2(g[i * c:(i + 1) * c]) for i in range(len(HGRN_MATMUL_LEVELS))]
        for m in HGRN_BCAST_LEVELS:
            segs = [-jnp.abs(c2[s0:s0 + 2 * m] - c_ref[slot, pl.ds(s0 + m - 1, 1), :]) for s0 in range(0, c, 2 * m)]
            wl.append(jnp.exp2(jnp.concatenate(segs, axis=0) if len(segs) > 1 else segs[0]))
        wlb = [x.astype(BF16) for x in wl]
        xs = [qb, qb * pre["f"].astype(BF16)] + [qb * x for x in wlb]
        ys = [kb, kb] + [kb * x for x in wlb]

        def scores(hs):
            a = None
            for p in range(HGRN_PRODUCTS):
                pp = lax.dot_general(xs[p][:, hs], ys[p][:, hs], nt, preferred_element_type=F32)
                a = jnp.where(sel == p, pp, 0.0 if a is None else a)
            return a

        finish(pre, scores)

    def group(i, carry):
        pres = [prepare(i * HGRN_UNROLL + u, u) for u in range(HGRN_UNROLL)]
        span = pres[0]["c_last"]
        for pre in pres[1:]:
            span = jnp.minimum(span, pre["c_last"])
        small = jnp.min(span) >= -HGRN_DIRECT_SPAN

        @pl.when(small)
        def _():
            for pre in pres:
                chunk_direct(pre)

        @pl.when(jnp.logical_not(small))
        def _():
            for pre in pres:
                chunk_dyadic(pre)

        return carry

    lax.fori_loop(0, nchunk // HGRN_UNROLL, group, 0)


def _hgrn_branch(proj, lb, norm_g, expo, sel, batch, seq, tl, nh):
    t = batch * seq
    nl = seq // tl
    gw = nh * HGRN_HEAD
    groups = D_BRANCH // gw
    col = lambda base: (lambda b, h, s: (b * nl + s, base * groups + h))
    vec = pl.BlockSpec((1, gw), lambda b, h, s: (0, h))
    return pl.pallas_call(
        functools.partial(_hgrn_body, nchunk=tl // HGRN_CHUNK, nh=nh),
        grid=(batch, groups, nl),
        in_specs=[
            pl.BlockSpec((tl, gw), col(COL_B_Q)),
            pl.BlockSpec((tl, gw), col(COL_B_F)),
            pl.BlockSpec((tl, gw), col(COL_B_I)),
            pl.BlockSpec((tl, gw), col(COL_B_GATE)),
            vec, vec,
            pl.BlockSpec(((1 + len(HGRN_MATMUL_LEVELS)) * HGRN_CHUNK, HGRN_CHUNK), lambda b, h, s: (0, 0)),
            pl.BlockSpec((HGRN_CHUNK, HGRN_CHUNK), lambda b, h, s: (0, 0)),
        ],
        out_specs=pl.BlockSpec((tl, gw), lambda b, h, s: (b * nl + s, h)),
        out_shape=jax.ShapeDtypeStruct((t, D_BRANCH), BF16),
        scratch_shapes=[pltpu.VMEM((nh, HGRN_HEAD, HGRN_HEAD), F32),
                        pltpu.VMEM((HGRN_UNROLL, HGRN_CHUNK, gw), F32)],
        compiler_params=_params("parallel", "parallel", "arbitrary"),
        name="hgrn_branch",
    )(proj, proj, proj, proj, lb, norm_g, expo, sel)


def _fox_constants():
    pair = np.zeros((LANES, LANES), np.float32)
    pair[:FOX_HEAD, :FOX_HEAD] = 1.0 / FOX_HEAD
    pair[FOX_HEAD:, FOX_HEAD:] = 1.0 / FOX_HEAD
    wide = FOX_HEADS * LANES
    place_q = np.zeros((LANES, wide), np.float32)
    place_k = np.zeros((LANES, wide), np.float32)
    one_v = np.zeros((1, wide), np.float32)
    for h in range(FOX_HEADS):
        aug = h * LANES + (FOX_HEAD if h % 2 == 0 else 0)
        one_v[0, aug] = 1.0
        for p in range(N_SPLIT):
            place_q[p * FOX_HEADS + h, aug + p] = 1.0
            place_q[AUG_ONE_LANE, aug + N_SPLIT + p] = 1.0
            place_k[AUG_ONE_LANE, aug + p] = 1.0
            place_k[p * FOX_HEADS + h, aug + N_SPLIT + p] = -1.0
    return pair, place_q, place_k, one_v


def _fox_pre_body(q_ref, k_ref, v_ref, fl_ref, fb_ref, qg_ref, kg_ref, pair_ref, pq_ref, pk_ref,
                  onev_ref, tri_ref, qo_ref, ko_ref, vo_ref, carry_ref, *, ts):
    @pl.when(pl.program_id(1) == 0)
    def _():
        carry_ref[...] = jnp.zeros((1, LANES), F32)

    ls = _log2_sigmoid(fl_ref[...] + fb_ref[...])
    tri = tri_ref[...]
    fcum = jnp.zeros((ts, LANES), F32)
    for p in _split_bf16(ls, N_SPLIT):
        fcum = fcum + jnp.dot(tri, p, preferred_element_type=F32)
    fcum = fcum + carry_ref[...]
    carry_ref[...] = fcum[ts - 1:ts, :]

    lane = lax.broadcasted_iota(jnp.int32, (ts, LANES), 1)
    p0, p1, p2 = _split_bf16(fcum, N_SPLIT)
    pieces = jnp.where(lane < FOX_HEADS, p0.astype(F32),
                       jnp.where(lane < 2 * FOX_HEADS, p1.astype(F32),
                                 jnp.where(lane < 3 * FOX_HEADS, p2.astype(F32),
                                           jnp.where(lane == AUG_ONE_LANE, 1.0, 0.0)))).astype(BF16)

    scale = LOG2E / math.sqrt(FOX_HEAD)
    low = lane < FOX_HEAD
    for src_ref, g_ref, place_ref, dst_ref, mul in ((q_ref, qg_ref, pq_ref, qo_ref, scale), (k_ref, kg_ref, pk_ref, ko_ref, 1.0)):
        for pr in range(FOX_HEADS // 2):
            sl = slice(pr * LANES, (pr + 1) * LANES)
            x = src_ref[:, sl].astype(F32)
            ms = jnp.dot((x * x).astype(BF16), pair_ref[...], preferred_element_type=F32)
            xn = x * lax.rsqrt(ms + EPS) * (g_ref[...] * mul)
            for h in (2 * pr, 2 * pr + 1):
                hs = slice(h * LANES, (h + 1) * LANES)
                aug = jnp.dot(pieces, place_ref[:, hs], preferred_element_type=F32)
                dst_ref[:, hs] = (jnp.where(low, xn, aug) if h % 2 == 0 else jnp.where(low, aug, xn)).astype(BF16)
    for pr in range(FOX_HEADS // 2):
        v = v_ref[:, pr * LANES:(pr + 1) * LANES].astype(F32)
        for h in (2 * pr, 2 * pr + 1):
            hs = slice(h * LANES, (h + 1) * LANES)
            one = jnp.broadcast_to(onev_ref[:, hs], (ts, LANES))
            vo_ref[:, hs] = (jnp.where(low, v, one) if h % 2 == 0 else jnp.where(low, one, v)).astype(BF16)


def _fox_prologue(proj, f_logit, f_bias, qn_g, kn_g, consts, batch, seq, ts):
    t = batch * seq
    ns = seq // ts
    pair, place_q, place_k, one_v, tri = consts
    wide = FOX_HEADS * LANES
    row = lambda b, s: b * ns + s
    const = lambda shape: pl.BlockSpec(shape, lambda b, s: (0, 0))
    wide_out = pl.BlockSpec((ts, wide), lambda b, s: (row(b, s), 0))
    return pl.pallas_call(
        functools.partial(_fox_pre_body, ts=ts),
        grid=(batch, ns),
        in_specs=[
            pl.BlockSpec((ts, D_BRANCH), lambda b, s: (row(b, s), COL_C_Q)),
            pl.BlockSpec((ts, D_BRANCH), lambda b, s: (row(b, s), COL_C_K)),
            pl.BlockSpec((ts, D_BRANCH), lambda b, s: (row(b, s), COL_C_V)),
            pl.BlockSpec((ts, LANES), lambda b, s: (row(b, s), 0)),
            const((1, LANES)), const((1, LANES)), const((1, LANES)),
            const((LANES, LANES)), const((LANES, wide)), const((LANES, wide)), const((1, wide)),
            const((ts, ts)),
        ],
        out_specs=[wide_out, wide_out, wide_out],
        out_shape=[jax.ShapeDtypeStruct((t, wide), BF16)] * 3,
        scratch_shapes=[pltpu.VMEM((1, LANES), F32)],
        compiler_params=_params("parallel", "arbitrary"),
        name="fox_prologue",
    )(proj, proj, proj, f_logit, f_bias, qn_g, kn_g, pair, place_q, place_k, one_v, tri)


def _fox_body(q_ref, k_ref, v_ref, gate_ref, o_ref, m_ref, acc_ref, *, tq, tk):
    qi = pl.program_id(2)
    nt = (((1,), (1,)), ((), ()))

    for hh in range(2):
        m_ref[hh] = jnp.full((tq, LANES), -jnp.inf, F32)
        acc_ref[hh] = jnp.zeros((tq, LANES), F32)

    def step(j, r0, nr, nkb=1, visible=None):
        k0 = pl.multiple_of(j * tk, tk)
        kw = nkb * tk
        rs = slice(r0, r0 + nr)
        for hh in range(2):
            hs = slice(hh * LANES, (hh + 1) * LANES)
            s = lax.dot_general(q_ref[rs, hs], k_ref[pl.ds(k0, kw), hs], nt, preferred_element_type=F32)
            if visible is not None:
                rows = lax.broadcasted_iota(jnp.int32, (tk, kw), 0)
                cols = lax.broadcasted_iota(jnp.int32, (tk, kw), 1)
                edge = jnp.where(cols <= rows + visible, s[:tk], -jnp.inf)
                s = edge if nr == tk else jnp.concatenate([edge, s[tk:]], axis=0)
            blocks = [s[:, cb * LANES:(cb + 1) * LANES] for cb in range(kw // LANES)]
            mx = blocks[0]
            for blk in blocks[1:]:
                mx = jnp.maximum(mx, blk)
            m_prev = m_ref[hh, rs]
            m_new = jnp.maximum(m_prev, jnp.max(mx, axis=-1, keepdims=True))
            alpha = jnp.exp2(m_prev - m_new)
            p = jnp.concatenate([jnp.exp2(blk - m_new).astype(BF16) for blk in blocks], axis=1)
            acc_ref[hh, rs] = alpha * acc_ref[hh, rs] + jnp.dot(p, v_ref[pl.ds(k0, kw), hs],
                                                                preferred_element_type=F32)
            m_ref[hh, rs] = m_new

    nsub = tq // tk

    def body(jj, carry):
        step(2 * jj, 0, tq, nkb=2)
        return carry

    nfull = qi * nsub
    lax.fori_loop(0, nfull // 2, body, 0)
    if nsub % 2 == 1:
        @pl.when(nfull % 2 == 1)
        def _():
            step(nfull - 1, 0, tq)

    for a in range(nsub):
        step(nfull + a, a * tk, tq - a * tk, visible=0)

    lane = lax.broadcasted_iota(jnp.int32, (tq, LANES), 1)
    acc0 = acc_ref[0]
    acc1 = acc_ref[1]
    out = jnp.where(lane < FOX_HEAD, acc0 / acc0[:, FOX_HEAD:FOX_HEAD + 1], acc1 / acc1[:, 0:1])
    o_ref[...] = (out * _silu(gate_ref[...].astype(F32))).astype(BF16)


def _fox_attention(qa, ka, va, proj, batch, seq, tq, tk):
    t = batch * seq
    nq = seq // tq
    pairs = FOX_HEADS // 2
    kv = pl.BlockSpec((seq, 2 * LANES), lambda b, p, i: (b, p))
    return pl.pallas_call(
        functools.partial(_fox_body, tq=tq, tk=tk),
        grid=(batch, pairs, nq),
        in_specs=[
            pl.BlockSpec((tq, 2 * LANES), lambda b, p, i: (b * nq + i, p)),
            kv, kv,
            pl.BlockSpec((tq, LANES), lambda b, p, i: (b * nq + i, COL_C_GATE * pairs + p)),
        ],
        out_specs=pl.BlockSpec((tq, LANES), lambda b, p, i: (b * nq + i, p)),
        out_shape=jax.ShapeDtypeStruct((t, D_BRANCH), BF16),
        scratch_shapes=[pltpu.VMEM((2, tq, LANES), F32)] * 2,
        compiler_params=_params("parallel", "parallel", "arbitrary"),
        name="fox_attention",
    )(qa, ka, va, proj)


def _merge_body(x_ref, ya_ref, yb_ref, yc_ref, ga_ref, gb_ref, gc_ref, wb_ref, wo_ref, o_ref):
    mixed = None
    for i, (y_ref, g_ref) in enumerate(((ya_ref, ga_ref), (yb_ref, gb_ref), (yc_ref, gc_ref))):
        d = jnp.dot(y_ref[...], wb_ref[i], preferred_element_type=F32)
        term = _sigmoid(g_ref[...].astype(F32)) * d
        mixed = term if mixed is None else mixed + term
    o_ref[...] = x_ref[...] + jnp.dot(mixed.astype(BF16), wo_ref[...], preferred_element_type=F32)


def _merge(x2, ya, yb, yc, proj, w_branch, w_out, tm):
    t = x2.shape[0]
    rows = lambda c: pl.BlockSpec((tm, D_MODEL), lambda m: (m, c))
    return pl.pallas_call(
        _merge_body,
        grid=(t // tm,),
        in_specs=[
            rows(0), rows(0), rows(0), rows(0),
            rows(COL_MERGE), rows(COL_MERGE + 1), rows(COL_MERGE + 2),
            pl.BlockSpec((N_BRANCH, D_BRANCH, D_MODEL), lambda m: (0, 0, 0)),
            pl.BlockSpec((D_MODEL, D_MODEL), lambda m: (0, 0)),
        ],
        out_specs=rows(0),
        out_shape=jax.ShapeDtypeStruct((t, D_MODEL), F32),
        compiler_params=_params("parallel"),
        name="merge_out",
    )(x2, ya, yb, yc, proj, proj, proj, w_branch, w_out)


def _tiles(seq):
    return dict(
        tm_in=min(2048, seq), tn_in=2048,
        ts_conv=min(256, seq),
        tl_hgrn=min(512, seq), nh_hgrn=4,
        ts_fox=min(512, seq),
        tq=min(1024, seq), tk=min(512, seq),
        tm_merge=min(512, seq),
    )


def kernel(x, norm_g, w_in, conv_w, conv_b, conv_ln_g, conv_ln_b, hgrn_lb_logits, hgrn_norm_g, fox_f_bias,
           fox_qn_g, fox_kn_g, w_branch, w_out):
    batch, seq, _ = x.shape
    depth = w_in.shape[0]
    tl = _tiles(seq)
    t = batch * seq

    p = jax.nn.softmax(hgrn_lb_logits.astype(F32), axis=0)
    cum = jnp.cumsum(p, axis=0)
    lower_bounds = cum - cum[0:1]

    expo_np, sel_np = _hgrn_constants()
    expo = jnp.asarray(expo_np, BF16)
    sel = jnp.asarray(sel_np)
    pair_np, pq_np, pk_np, onev_np = _fox_constants()
    ts = tl["ts_fox"]
    tri = jnp.asarray(np.tril(np.ones((ts, ts), np.float32)), BF16)
    fox_consts = (jnp.asarray(pair_np, BF16), jnp.asarray(pq_np, BF16), jnp.asarray(pk_np, BF16),
                  jnp.asarray(onev_np, F32), tri)

    x2 = x.reshape(t, D_MODEL)
    for l in range(depth):
        w = w_in[l]
        w_main = jnp.concatenate([w[:, :C_F_OFFSET], w[:, C_F_OFFSET + FOX_HEADS:]], axis=1).astype(BF16)
        wf = w[:, C_F_OFFSET:C_F_OFFSET + FOX_HEADS]
        w_f = jnp.concatenate([wf] * N_SPLIT + [jnp.zeros((D_MODEL, LANES - N_SPLIT * FOX_HEADS), F32)],
                              axis=1).astype(BF16)
        fb = jnp.concatenate([fox_f_bias[l]] * N_SPLIT + [jnp.zeros((LANES - N_SPLIT * FOX_HEADS,), F32)])[None, :]
        qg = jnp.tile(fox_qn_g[l], 2)[None, :]
        kg = jnp.tile(fox_kn_g[l], 2)[None, :]
        cw = jnp.broadcast_to(conv_w[l][:, None, :], (CONV_K, SUBLANES, D_BRANCH))

        proj, f_logit = _inproj(x2, norm_g[l][None, :], w_main, w_f, tl["tm_in"], tl["tn_in"])
        ya = _conv_branch(proj, cw, conv_b[l][None, :], conv_ln_g[l][None, :], conv_ln_b[l][None, :],
                          batch, seq, tl["ts_conv"])
        yb = _hgrn_branch(proj, lower_bounds[l][None, :], hgrn_norm_g[l][None, :], expo, sel, batch, seq,
                          tl["tl_hgrn"], tl["nh_hgrn"])
        qa, ka, va = _fox_prologue(proj, f_logit, fb, qg, kg, fox_consts, batch, seq, ts)
        yc = _fox_attention(qa, ka, va, proj, batch, seq, tl["tq"], tl["tk"])
        x2 = _merge(x2, ya, yb, yc, proj, w_branch[l].astype(BF16), w_out[l].astype(BF16), tl["tm_merge"])
    return x2.reshape(batch, seq, D_MODEL)
```
